```python
import jax, jax.numpy as jnp
from jax import lax
import numpy as np

D_MODEL = 4096
BATCH = 2
SEQ = 4096
DEPTH = 2

CTX_LEN = 256
GRID_W = 64
HEAD_DIM = 128
N_Q_HEADS = 16
N_KV_HEADS = 4
GQA_GROUP = N_Q_HEADS // N_KV_HEADS
ATTN_W = N_Q_HEADS * HEAD_DIM
KV_W = N_KV_HEADS * HEAD_DIM
AXIS_DIM = HEAD_DIM // 2
ROPE_THETA = 10000.0
Q_BLOCK = 128
CONV_W = 1024
CONV_K = 3
GM_W = 1024
GM_GROUPS = 8
GM_GROUP_W = GM_W // GM_GROUPS
GM_CHUNK = 128
N_BRANCH = 3
EPS = 1e-6
ALPHA = (2.0 * DEPTH) ** 0.25
BETA = (8.0 * DEPTH) ** -0.25

IN_SPLITS = (ATTN_W, KV_W, KV_W, ATTN_W,
             CONV_W, CONV_W, CONV_W, CONV_W,
             GM_W, GM_W, GM_W,
             N_BRANCH * D_MODEL)
IN_COLS = sum(IN_SPLITS)
IN_SPLIT_IDX = tuple(int(i) for i in np.cumsum(IN_SPLITS)[:-1])
KV_START = ATTN_W
KV_END = ATTN_W + 2 * KV_W

kernel_name = "hybrid_parallel_gqa_shortconv_gmlp_dit"


def layer_norm(x, g, b):
    xf = x.astype(jnp.float32)
    mu = jnp.mean(xf, axis=-1, keepdims=True)
    var = jnp.mean(jnp.square(xf - mu), axis=-1, keepdims=True)
    return ((xf - mu) * lax.rsqrt(var + EPS) * g + b).astype(x.dtype)


def rms_norm(x, g):
    xf = x.astype(jnp.float32)
    return (xf * lax.rsqrt(jnp.mean(jnp.square(xf), axis=-1, keepdims=True) + EPS) * g).astype(x.dtype)


def rope_2d_tables(rows):
    t = jnp.arange(rows * GRID_W)
    pos = jnp.stack([(t // GRID_W), (t % GRID_W)], axis=-1).astype(jnp.float32)
    freqs = ROPE_THETA ** (-jnp.arange(0, AXIS_DIM, 2, dtype=jnp.float32) / AXIS_DIM)
    ang = pos[:, :, None] * freqs
    return jnp.cos(ang), jnp.sin(ang)


def apply_rope_2d(x, cos, sin):
    b, n, h, _ = x.shape
    xa = x.reshape(b, n, h, 2, AXIS_DIM).astype(jnp.float32)
    half = AXIS_DIM // 2
    x1, x2 = xa[..., :half], xa[..., half:]
    cs, sn = cos[None, :, None], sin[None, :, None]
    out = jnp.concatenate([x1 * cs - x2 * sn, x2 * cs + x1 * sn], axis=-1)
    return out.reshape(b, n, h, HEAD_DIM).astype(x.dtype)


def modulation(cond, w_ada, b_ada):
    return jnp.split(jax.nn.silu(cond) @ w_ada + b_ada, 3, axis=-1)


def q_heads(q, q_norm, rope):
    b, n, _ = q.shape
    q = rms_norm(q.reshape(b, n, N_Q_HEADS, HEAD_DIM), q_norm)
    return q if rope is None else apply_rope_2d(q, *rope)


def kv_heads(k, v, k_norm, rope):
    b, n, _ = k.shape
    k = rms_norm(k.reshape(b, n, N_KV_HEADS, HEAD_DIM), k_norm)
    v = v.reshape(b, n, N_KV_HEADS, HEAD_DIM)
    return (k if rope is None else apply_rope_2d(k, *rope)), v


def gqa_attention(q, k, v):
    b, n, _, _ = q.shape
    nb = n // Q_BLOCK
    scale = HEAD_DIM ** -0.5
    qb = q.reshape(b, nb, Q_BLOCK, N_KV_HEADS, GQA_GROUP, HEAD_DIM).transpose(1, 0, 2, 3, 4, 5)

    def block(qblk):
        s = jnp.einsum('bqkgd,bskd->bkgqs', qblk, k, preferred_element_type=jnp.float32) * scale
        p = jax.nn.softmax(s, axis=-1).astype(v.dtype)
        return jnp.einsum('bkgqs,bskd->bqkgd', p, v)

    o = lax.map(block, qb)
    return o.transpose(1, 0, 2, 3, 4, 5).reshape(b, n, ATTN_W)


def short_gated_conv(b_gate, c_gate, h, conv_w):
    z = c_gate * h
    n = z.shape[1]
    zp = jnp.pad(z, ((0, 0), (1, 1), (0, 0)))
    y = conv_w[0] * zp[:, :n] + conv_w[1] * zp[:, 1:n + 1] + conv_w[2] * zp[:, 2:]
    return b_gate * y


def spatial_gating(u, v, ln_g, ln_b, ws, bias):
    b, n, _ = v.shape
    vn = layer_norm(v, ln_g, ln_b).reshape(b, n // GM_CHUNK, GM_CHUNK, GM_GROUPS, GM_GROUP_W)
    s = jnp.einsum('gpq,bcqgd->bcpgd', ws, vn) + bias.T[None, None, :, :, None]
    return u * s.reshape(b, n, GM_W)


def mixer_sublayer(parts, q, k, v, conv_w, gm_ln_g, gm_ln_b, gm_ws, gm_b,
                   w_br_attn, w_br_conv, w_br_gm, w_out):
    _, _, _, a_gate, c_b, c_c, c_h, c_gate, g_u, g_v, g_gate, m_gate = parts
    y_attn = gqa_attention(q, k, v) * jax.nn.silu(a_gate)
    y_conv = short_gated_conv(c_b, c_c, c_h, conv_w) * jax.nn.silu(c_gate)
    y_gm = spatial_gating(g_u, g_v, gm_ln_g, gm_ln_b, gm_ws, gm_b) * jax.nn.silu(g_gate)
    g_attn, g_conv, g_gm = jnp.split(jax.nn.sigmoid(m_gate), N_BRANCH, axis=-1)
    merged = (g_attn * (y_attn @ w_br_attn) + g_conv * (y_conv @ w_br_conv)
              + g_gm * (y_gm @ w_br_gm))
    return merged @ w_out


def setup_inputs(seed: int = 0) -> dict:
    key = jax.random.key(seed)
    ks = jax.random.split(key, 24)
    f32 = jnp.float32
    nrm = lambda k, shape, s: jax.random.normal(k, shape, f32) * s
    return {
        "x": nrm(ks[0], (BATCH, SEQ, D_MODEL), 1.0),
        "c": nrm(ks[1], (BATCH, D_MODEL), 1.0),
        "ctx": nrm(ks[2], (BATCH, CTX_LEN, D_MODEL), 1.0),
        "c_ctx": nrm(ks[3], (D_MODEL,), 1.0),
        "w_ada": nrm(ks[4], (DEPTH, D_MODEL, 3 * D_MODEL), D_MODEL ** -0.5),
        "b_ada": nrm(ks[5], (DEPTH, 3 * D_MODEL), 0.01),
        "w_in": nrm(ks[6], (DEPTH, D_MODEL, IN_COLS), D_MODEL ** -0.5),
        "q_norm": 1.0 + nrm(ks[7], (DEPTH, HEAD_DIM), 0.02),
        "k_norm": 1.0 + nrm(ks[8], (DEPTH, HEAD_DIM), 0.02),
        "conv_w": nrm(ks[9], (DEPTH, CONV_K, CONV_W), CONV_K ** -0.5),
        "gm_ln_g": 1.0 + nrm(ks[10], (DEPTH, GM_W), 0.02),
        "gm_ln_b": nrm(ks[11], (DEPTH, GM_W), 0.02),
        "gm_ws": nrm(ks[12], (DEPTH, GM_GROUPS, GM_CHUNK, GM_CHUNK), GM_CHUNK ** -0.5),
        "gm_b": 1.0 + nrm(ks[13], (DEPTH, GM_GROUPS, GM_CHUNK), 0.02),
        "w_br_attn": nrm(ks[14], (DEPTH, ATTN_W, D_MODEL), ATTN_W ** -0.5),
        "w_br_conv": nrm(ks[15], (DEPTH, CONV_W, D_MODEL), CONV_W ** -0.5),
        "w_br_gm": nrm(ks[16], (DEPTH, GM_W, D_MODEL), GM_W ** -0.5),
        "w_out": nrm(ks[17], (DEPTH, D_MODEL, D_MODEL), BETA * D_MODEL ** -0.5),
        "ln_g": 1.0 + nrm(ks[18], (DEPTH, D_MODEL), 0.02),
        "ln_b": nrm(ks[19], (DEPTH, D_MODEL), 0.02),
    }


def reference(x, c, ctx, c_ctx, w_ada, b_ada, w_in, q_norm, k_norm, conv_w, gm_ln_g, gm_ln_b,
              gm_ws, gm_b, w_br_attn, w_br_conv, w_br_gm, w_out, ln_g, ln_b):
    n = x.shape[1]
    rows = n // GRID_W
    rope = rope_2d_tables(rows)
    for l in range(DEPTH):
        last = l == DEPTH - 1
        sh_c, sc_c, gt_c = modulation(c_ctx, w_ada[l], b_ada[l])
        u_ctx = ctx * (1.0 + sc_c) + sh_c
        if last:
            k_raw, v_raw = jnp.split(u_ctx @ w_in[l][:, KV_START:KV_END], 2, axis=-1)
            k_c, v_c = kv_heads(k_raw, v_raw, k_norm[l], None)
        else:
            p_ctx = jnp.split(u_ctx @ w_in[l], IN_SPLIT_IDX, axis=-1)
            q_c = q_heads(p_ctx[0], q_norm[l], None)
            k_c, v_c = kv_heads(p_ctx[1], p_ctx[2], k_norm[l], None)
            out_c = mixer_sublayer(p_ctx, q_c, k_c, v_c, conv_w[l], gm_ln_g[l], gm_ln_b[l], gm_ws[l],
                                   gm_b[l], w_br_attn[l], w_br_conv[l], w_br_gm[l], w_out[l])
            new_ctx = layer_norm(ALPHA * ctx + gt_c * out_c, ln_g[l], ln_b[l])
        sh, sc, gt = modulation(c, w_ada[l], b_ada[l])
        u = x * (1.0 + sc[:, None]) + sh[:, None]
        p = jnp.split(u @ w_in[l], IN_SPLIT_IDX, axis=-1)
        q = q_heads(p[0], q_norm[l], rope)
        k, v = kv_heads(p[1], p[2], k_norm[l], rope)
        k_all = jnp.concatenate([k_c, k], axis=1)
        v_all = jnp.concatenate([v_c, v], axis=1)
        out = mixer_sublayer(p, q, k_all, v_all, conv_w[l], gm_ln_g[l], gm_ln_b[l], gm_ws[l],
                             gm_b[l], w_br_attn[l], w_br_conv[l], w_br_gm[l], w_out[l])
        x = layer_norm(ALPHA * x + gt[:, None] * out, ln_g[l], ln_b[l])
        if not last:
            ctx = new_ctx
    return x
```

```python
import functools

import jax
import jax.numpy as jnp
from jax import lax
from jax.experimental import pallas as pl
from jax.experimental.pallas import tpu as pltpu

D_MODEL = 4096
DEPTH = 2
GRID_W = 64
HEAD_DIM = 128
N_Q_HEADS = 16
N_KV_HEADS = 4
GQA_GROUP = N_Q_HEADS // N_KV_HEADS
ATTN_W = N_Q_HEADS * HEAD_DIM
KV_W = N_KV_HEADS * HEAD_DIM
GROUP_W = GQA_GROUP * HEAD_DIM
AXIS_DIM = HEAD_DIM // 2
ROPE_THETA = 10000.0
CONV_W = 1024
GM_W = 1024
GM_GROUPS = 8
GM_GROUP_W = GM_W // GM_GROUPS
GM_CHUNK = 128
N_BRANCH = 3
EPS = 1e-6
ALPHA = (2.0 * DEPTH) ** 0.25

COL_Q = 0
COL_K = ATTN_W
COL_V = COL_K + KV_W
COL_AGATE = COL_V + KV_W
COL_CB = COL_AGATE + ATTN_W
COL_CC = COL_CB + CONV_W
COL_CH = COL_CC + CONV_W
COL_CGATE = COL_CH + CONV_W
COL_GU = COL_CGATE + CONV_W
COL_GV = COL_GU + GM_W
COL_GGATE = COL_GV + GM_W
COL_MGATE = COL_GGATE + GM_W
IN_COLS = COL_MGATE + N_BRANCH * D_MODEL

MOD_ROWS = 8
V7X_VMEM_LIMIT = 56 * 1024 * 1024

BF16 = jnp.bfloat16
F32 = jnp.float32


def _params(sem):
    return pltpu.CompilerParams(dimension_semantics=sem, vmem_limit_bytes=V7X_VMEM_LIMIT)


def _silu(x):
    return x * jax.nn.sigmoid(x)


def _modulation_body(c_ref, w_ref, b_ref, o_ref):
    s = _silu(c_ref[...]).astype(BF16)
    o_ref[...] = jnp.dot(s, w_ref[...], preferred_element_type=F32) + b_ref[...]


def _modulation(cond, w_ada, b_ada, tn=512):
    n = 3 * D_MODEL
    return pl.pallas_call(
        _modulation_body,
        grid=(DEPTH, n // tn),
        in_specs=[
            pl.BlockSpec((MOD_ROWS, D_MODEL), lambda l, j: (0, 0)),
            pl.BlockSpec((None, D_MODEL, tn), lambda l, j: (l, 0, j)),
            pl.BlockSpec((None, 1, tn), lambda l, j: (l, 0, j)),
        ],
        out_specs=pl.BlockSpec((None, MOD_ROWS, tn), lambda l, j: (l, 0, j)),
        out_shape=jax.ShapeDtypeStruct((DEPTH, MOD_ROWS, n), F32),
        compiler_params=_params(("arbitrary", "arbitrary")),
        name="modulation",
    )(cond, w_ada, b_ada.reshape(DEPTH, 1, n))


def _ln_mod_body(*refs, do_ln, emit_x, emit_u):
    it = iter(refs)
    x_ref = next(it)
    g_ref = b_ref = sc_ref = sh_ref = None
    if do_ln:
        g_ref, b_ref = next(it), next(it)
    if emit_u:
        sc_ref, sh_ref = next(it), next(it)
    y = x_ref[...]
    if do_ln:
        mu = jnp.mean(y, axis=-1, keepdims=True)
        d = y - mu
        var = jnp.mean(d * d, axis=-1, keepdims=True)
        y = d * lax.rsqrt(var + EPS) * g_ref[...] + b_ref[...]
    if emit_x:
        next(it)[...] = y
    if emit_u:
        next(it)[...] = (y * (1.0 + sc_ref[...]) + sh_ref[...]).astype(BF16)


def _ln_mod(x, row_fn, ln=None, mod=None, emit_x=True, tr=256):
    m = x.shape[0]
    do_ln, emit_u = ln is not None, mod is not None
    row = pl.BlockSpec((tr, D_MODEL), lambda i: (i, 0))
    vec = pl.BlockSpec((1, D_MODEL), lambda i: (0, 0))
    mvec = pl.BlockSpec((None, 1, D_MODEL), lambda i: (row_fn(i * tr), 0, 0))
    ins, specs = [x], [row]
    if do_ln:
        ins += [ln[0].reshape(1, D_MODEL), ln[1].reshape(1, D_MODEL)]
        specs += [vec, vec]
    if emit_u:
        ins += [mod[0], mod[1]]
        specs += [mvec, mvec]
    out_shape, out_specs = [], []
    if emit_x:
        out_shape.append(jax.ShapeDtypeStruct((m, D_MODEL), F32))
        out_specs.append(row)
    if emit_u:
        out_shape.append(jax.ShapeDtypeStruct((m, D_MODEL), BF16))
        out_specs.append(row)
    return pl.pallas_call(
        functools.partial(_ln_mod_body, do_ln=do_ln, emit_x=emit_x, emit_u=emit_u),
        grid=(m // tr,),
        in_specs=specs,
        out_specs=out_specs,
        out_shape=out_shape,
        compiler_params=_params(("arbitrary",)),
        name="ln_mod",
    )(*ins)


def _proj_body(u_ref, *refs, n_w, n_extra, epilogue):
    w_refs = refs[:n_w]
    extra_refs = refs[n_w:n_w + n_extra]
    out_refs = refs[n_w + n_extra:]
    u = u_ref[...]
    accs = [jnp.dot(u, w[...], preferred_element_type=F32) for w in w_refs]
    epilogue(accs, extra_refs, out_refs)


def _proj(u, w, layer, cols, width, tm, tn, epilogue, n_out, extras=(), name="proj"):
    m, k = u.shape
    in_specs = [pl.BlockSpec((tm, k), lambda i, j: (i, 0))]
    for c0 in cols:
        in_specs.append(pl.BlockSpec((None, k, tn), lambda i, j, cb=c0 // tn: (layer, 0, cb + j)))
    in_specs += [spec for _, spec in extras]
    return pl.pallas_call(
        functools.partial(_proj_body, n_w=len(cols), n_extra=len(extras), epilogue=epilogue),
        grid=(m // tm, width // tn),
        in_specs=in_specs,
        out_specs=[pl.BlockSpec((tm, tn), lambda i, j: (i, j))] * n_out,
        out_shape=[jax.ShapeDtypeStruct((m, width), BF16)] * n_out,
        compiler_params=_params(("arbitrary", "arbitrary")),
        name=name,
    )(u, *([w] * len(cols)), *[a for a, _ in extras])


def _epi_identity(accs, extras, outs):
    outs[0][...] = accs[0].astype(BF16)


def _epi_silu(accs, extras, outs):
    outs[0][...] = _silu(accs[0]).astype(BF16)


def _epi_sigmoid(accs, extras, outs):
    outs[0][...] = jax.nn.sigmoid(accs[0]).astype(BF16)


def _epi_conv(accs, extras, outs):
    c_b, c_c, c_h, c_gate = accs
    outs[0][...] = (c_c * c_h).astype(BF16)
    outs[1][...] = (c_b * _silu(c_gate)).astype(BF16)


def _epi_gm(accs, extras, outs):
    g_u, g_v, g_gate = accs
    outs[0][...] = (g_u * _silu(g_gate)).astype(BF16)
    outs[1][...] = g_v.astype(BF16)


def _epi_norm_rope(accs, extras, outs, rope, scale):
    acc = accs[0]
    gain = extras[0][...]
    if rope:
        cos, sin = extras[1][...], extras[2][...]
        lane = lax.broadcasted_iota(jnp.int32, cos.shape, 1)
        first_half = (lane % AXIS_DIM) < (AXIS_DIM // 2)
    for h in range(acc.shape[1] // HEAD_DIM):
        xh = acc[:, h * HEAD_DIM:(h + 1) * HEAD_DIM]
        ms = jnp.mean(xh * xh, axis=-1, keepdims=True)
        xn = xh * lax.rsqrt(ms + EPS) * gain
        if rope:
            swapped = jnp.where(first_half,
                                pltpu.roll(xn, HEAD_DIM - AXIS_DIM // 2, 1),
                                pltpu.roll(xn, AXIS_DIM // 2, 1))
            xn = xn * cos + swapped * sin
        outs[0][:, h * HEAD_DIM:(h + 1) * HEAD_DIM] = (xn * scale).astype(BF16)


def _rope_tables(n):
    t = jnp.arange(n)
    pos = jnp.stack([t // GRID_W, t % GRID_W], axis=-1).astype(F32)
    freqs = ROPE_THETA ** (-jnp.arange(0, AXIS_DIM, 2, dtype=F32) / AXIS_DIM)
    ang = pos[:, :, None] * freqs
    cos, sin = jnp.cos(ang), jnp.sin(ang)
    cos = jnp.concatenate([cos, cos], axis=-1).reshape(n, HEAD_DIM)
    sin = jnp.concatenate([-sin, sin], axis=-1).reshape(n, HEAD_DIM)
    return cos, sin


def _attn_body(q_ref, k_ref, v_ref, gate_ref, o_ref, *, tq):
    q = q_ref[...]
    qs = jnp.concatenate([q[:, g * HEAD_DIM:(g + 1) * HEAD_DIM] for g in range(GQA_GROUP)], axis=0)
    s = lax.dot_general(qs, k_ref[...], (((1,), (1,)), ((), ())), preferred_element_type=F32)
    m = jnp.max(s, axis=-1, keepdims=True)
    p = jnp.exp(s - m)
    l = jnp.sum(p, axis=-1, keepdims=True)
    o = jnp.dot(p.astype(BF16), v_ref[...], preferred_element_type=F32) / l
    for g in range(GQA_GROUP):
        sl = slice(g * HEAD_DIM, (g + 1) * HEAD_DIM)
        o_ref[:, sl] = (o[g * tq:(g + 1) * tq] * gate_ref[:, sl].astype(F32)).astype(BF16)


def _attention(q, k, v, gate, batch, tq=128):
    sq = q.shape[0] // batch
    skv = k.shape[0] // batch
    nq = sq // tq
    qspec = pl.BlockSpec((tq, GROUP_W), lambda b, h, i: (b * nq + i, h))
    kspec = pl.BlockSpec((skv, HEAD_DIM), lambda b, h, i: (b, h))
    return pl.pallas_call(
        functools.partial(_attn_body, tq=tq),
        grid=(batch, N_KV_HEADS, nq),
        in_specs=[qspec, kspec, kspec, qspec],
        out_specs=qspec,
        out_shape=jax.ShapeDtypeStruct(q.shape, BF16),
        compiler_params=_params(("arbitrary", "arbitrary", "arbitrary")),
        name="attention",
    )(q, k, v, gate)


def _mix_body(z_ref, zp_ref, zn_ref, cg_ref, ug_ref, gv_ref, cw_ref, lg_ref, lb_ref, ws_ref, bias_ref,
              yc_ref, yg_ref, *, tr, tiles_per_seq):
    i = pl.program_id(0)
    z = z_ref[...].astype(F32)
    rows = lax.broadcasted_iota(jnp.int32, z.shape, 0)
    prev_row = jnp.where(i % tiles_per_seq == 0, 0.0, zp_ref[7:8, :].astype(F32))
    next_row = jnp.where((i + 1) % tiles_per_seq == 0, 0.0, zn_ref[0:1, :].astype(F32))
    z_prev = jnp.where(rows == 0, prev_row, pltpu.roll(z, 1, 0))
    z_next = jnp.where(rows == tr - 1, next_row, pltpu.roll(z, tr - 1, 0))
    conv = cw_ref[0:1, :] * z_prev + cw_ref[1:2, :] * z + cw_ref[2:3, :] * z_next
    yc_ref[...] = (cg_ref[...].astype(F32) * conv).astype(BF16)

    for c in range(tr // GM_CHUNK):
        rs = slice(c * GM_CHUNK, (c + 1) * GM_CHUNK)
        gv = gv_ref[rs, :].astype(F32)
        mu = jnp.mean(gv, axis=-1, keepdims=True)
        d = gv - mu
        var = jnp.mean(d * d, axis=-1, keepdims=True)
        vn = (d * lax.rsqrt(var + EPS) * lg_ref[...] + lb_ref[...]).astype(BF16)
        for g in range(GM_GROUPS):
            cs = slice(g * GM_GROUP_W, (g + 1) * GM_GROUP_W)
            s = jnp.dot(ws_ref[g], vn[:, cs], preferred_element_type=F32) + bias_ref[g]
            yg_ref[rs, cs] = (ug_ref[rs, cs].astype(F32) * s).astype(BF16)


def _mix(z, cgate, ug, gv, conv_w, ln_g, ln_b, ws, bias, seq, tr):
    m = z.shape[0]
    row = pl.BlockSpec((tr, CONV_W), lambda i: (i, 0))
    r8 = tr // 8
    prev8 = pl.BlockSpec((8, CONV_W), lambda i: (jnp.maximum(i * r8 - 1, 0), 0))
    next8 = pl.BlockSpec((8, CONV_W), lambda i: (jnp.minimum((i + 1) * r8, m // 8 - 1), 0))
    vec = pl.BlockSpec((1, GM_W), lambda i: (0, 0))
    full3 = pl.BlockSpec((GM_GROUPS, GM_CHUNK, GM_CHUNK), lambda i: (0, 0, 0))
    return pl.pallas_call(
        functools.partial(_mix_body, tr=tr, tiles_per_seq=seq // tr),
        grid=(m // tr,),
        in_specs=[row, prev8, next8, row, row, row,
                  pl.BlockSpec((3, CONV_W), lambda i: (0, 0)), vec, vec, full3, full3],
        out_specs=[row, row],
        out_shape=[jax.ShapeDtypeStruct((m, CONV_W), BF16), jax.ShapeDtypeStruct((m, GM_W), BF16)],
        compiler_params=_params(("arbitrary",)),
        name="conv_gmlp",
    )(z, z, z, cgate, ug, gv, conv_w, ln_g.reshape(1, GM_W), ln_b.reshape(1, GM_W), ws, bias)


def _merge_body(ya_ref, yc_ref, yg_ref, wa_ref, wc_ref, wg_ref, ga_ref, gc_ref, gg_ref, o_ref):
    acc = ga_ref[...].astype(F32) * jnp.dot(ya_ref[...], wa_ref[...], preferred_element_type=F32)
    acc += gc_ref[...].astype(F32) * jnp.dot(yc_ref[...], wc_ref[...], preferred_element_type=F32)
    acc += gg_ref[...].astype(F32) * jnp.dot(yg_ref[...], wg_ref[...], preferred_element_type=F32)
    o_ref[...] = acc.astype(BF16)


def _merge(ya, yc, yg, w_a, w_c, w_g, gates, layer, tm, tn=512):
    m = ya.shape[0]
    nj = D_MODEL // tn

    def lhs(width):
        return pl.BlockSpec((tm, width), lambda i, j: (i, 0))

    def wspec(width):
        return pl.BlockSpec((None, width, tn), lambda i, j: (layer, 0, j))

    def gspec(branch):
        return pl.BlockSpec((tm, tn), lambda i, j: (i, branch * nj + j))

    return pl.pallas_call(
        _merge_body,
        grid=(m // tm, nj),
        in_specs=[lhs(ATTN_W), lhs(CONV_W), lhs(GM_W), wspec(ATTN_W), wspec(CONV_W), wspec(GM_W),
                  gspec(0), gspec(1), gspec(2)],
        out_specs=pl.BlockSpec((tm, tn), lambda i, j: (i, j)),
        out_shape=jax.ShapeDtypeStruct((m, D_MODEL), BF16),
        compiler_params=_params(("arbitrary", "arbitrary")),
        name="merge",
    )(ya, yc, yg, w_a, w_c, w_g, gates, gates, gates)


def _outproj_body(m_ref, w_ref, x_ref, gt_ref, o_ref):
    acc = jnp.dot(m_ref[...], w_ref[...], preferred_element_type=F32)
    o_ref[...] = ALPHA * x_ref[...] + gt_ref[...] * acc


def _outproj(merged, w_out, x, gt, layer, row_fn, tm, tn=512):
    m = merged.shape[0]
    return pl.pallas_call(
        _outproj_body,
        grid=(m // tm, D_MODEL // tn),
        in_specs=[
            pl.BlockSpec((tm, D_MODEL), lambda i, j: (i, 0)),
            pl.BlockSpec((None, D_MODEL, tn), lambda i, j: (layer, 0, j)),
            pl.BlockSpec((tm, tn), lambda i, j: (i, j)),
            pl.BlockSpec((None, 1, tn), lambda i, j: (row_fn(i * tm), 0, j)),
        ],
        out_specs=pl.BlockSpec((tm, tn), lambda i, j: (i, j)),
        out_shape=jax.ShapeDtypeStruct((m, D_MODEL), F32),
        compiler_params=_params(("arbitrary", "arbitrary")),
        name="outproj",
    )(merged, w_out, x, gt)


def _qkv(u, w_in, layer, q_norm, k_norm, rope, seq, tm, want_q=True):
    tiles_per_seq = seq // tm
    gain = pl.BlockSpec((1, HEAD_DIM), lambda i, j: (0, 0))
    tab = pl.BlockSpec((tm, HEAD_DIM), lambda i, j: (i % tiles_per_seq, 0))

    def extras(g):
        ex = [(g.reshape(1, HEAD_DIM), gain)]
        if rope is not None:
            ex += [(rope[0], tab), (rope[1], tab)]
        return ex

    q = None
    if want_q:
        epi_q = functools.partial(_epi_norm_rope, rope=rope is not None, scale=HEAD_DIM ** -0.5)
        q, = _proj(u, w_in, layer, [COL_Q], ATTN_W, tm, 512, epi_q, 1, extras(q_norm[layer]), name="proj_q")
    epi_k = functools.partial(_epi_norm_rope, rope=rope is not None, scale=1.0)
    k, = _proj(u, w_in, layer, [COL_K], KV_W, tm, 512, epi_k, 1, extras(k_norm[layer]), name="proj_k")
    v, = _proj(u, w_in, layer, [COL_V], KV_W, tm, 512, _epi_identity, 1, name="proj_v")
    return q, k, v


def _mixer(u, q, k_all, v_all, wts, layer, batch, seq, tm):
    w_in = wts["w_in"]
    agate, = _proj(u, w_in, layer, [COL_AGATE], ATTN_W, tm, 512, _epi_silu, 1, name="proj_agate")
    z, cgate = _proj(u, w_in, layer, [COL_CB, COL_CC, COL_CH, COL_CGATE], CONV_W, tm, 256, _epi_conv, 2,
                     name="proj_conv")
    ug, gv = _proj(u, w_in, layer, [COL_GU, COL_GV, COL_GGATE], GM_W, tm, 256, _epi_gm, 2, name="proj_gm")
    gates, = _proj(u, w_in, layer, [COL_MGATE], N_BRANCH * D_MODEL, tm, 512, _epi_sigmoid, 1,
                   name="proj_mgate")
    ya = _attention(q, k_all, v_all, agate, batch)
    yc, yg = _mix(z, cgate, ug, gv, wts["conv_w"][layer], wts["gm_ln_g"][layer], wts["gm_ln_b"][layer],
                  wts["gm_ws"][layer], wts["gm_bias"][layer], seq, min(tm, 512))
    return _merge(ya, yc, yg, wts["w_br_attn"], wts["w_br_conv"], wts["w_br_gm"], gates, layer, tm)


def kernel(x, c, ctx, c_ctx, w_ada, b_ada, w_in, q_norm, k_norm, conv_w, gm_ln_g, gm_ln_b, gm_ws, gm_b,
           w_br_attn, w_br_conv, w_br_gm, w_out, ln_g, ln_b):
    batch, seq, _ = x.shape
    ctx_len = ctx.shape[1]
    tm, tm_c = 1024, ctx_len

    wts = {
        "w_in": w_in.astype(BF16), "w_br_attn": w_br_attn.astype(BF16), "w_br_conv": w_br_conv.astype(BF16),
        "w_br_gm": w_br_gm.astype(BF16), "conv_w": conv_w, "gm_ln_g": gm_ln_g, "gm_ln_b": gm_ln_b,
        "gm_ws": gm_ws.astype(BF16),
        "gm_bias": jnp.broadcast_to(gm_b[:, :, :, None], gm_b.shape + (GM_GROUP_W,)),
    }
    w_out_b = w_out.astype(BF16)
    w_ada_b = w_ada.astype(BF16)

    cond = jnp.zeros((MOD_ROWS, D_MODEL), F32).at[:batch].set(c).at[batch].set(c_ctx)
    mod = _modulation(cond, w_ada_b, b_ada)
    mod = mod.reshape(DEPTH, MOD_ROWS, 1, 3, D_MODEL)
    sh, sc, gt = mod[:, :, :, 0], mod[:, :, :, 1], mod[:, :, :, 2]

    lat_row = lambda r: r // seq
    ctx_row = lambda r: batch

    rope = _rope_tables(seq)
    xr = x.reshape(batch * seq, D_MODEL)
    cr = ctx.reshape(batch * ctx_len, D_MODEL)

    u, = _ln_mod(xr, lat_row, mod=(sc[0], sh[0]), emit_x=False)
    u_c, = _ln_mod(cr, ctx_row, mod=(sc[0], sh[0]), emit_x=False)

    for layer in range(DEPTH):
        last = layer == DEPTH - 1
        q_c, k_c, v_c = _qkv(u_c, wts["w_in"], layer, q_norm, k_norm, None, ctx_len, tm_c, want_q=not last)
        if not last:
            merged_c = _mixer(u_c, q_c, k_c, v_c, wts, layer, batch, ctx_len, tm_c)
            pre_c = _outproj(merged_c, w_out_b, cr, gt[layer], layer, ctx_row, tm_c)
            cr, u_c = _ln_mod(pre_c, ctx_row, ln=(ln_g[layer], ln_b[layer]),
                              mod=(sc[layer + 1], sh[layer + 1]))
        q, k, v = _qkv(u, wts["w_in"], layer, q_norm, k_norm, rope, seq, tm)
        k_all = jnp.concatenate([k.reshape(batch, seq, KV_W), k_c.reshape(batch, ctx_len, KV_W)], axis=1)
        v_all = jnp.concatenate([v.reshape(batch, seq, KV_W), v_c.reshape(batch, ctx_len, KV_W)], axis=1)
        k_all = k_all.reshape(batch * (seq + ctx_len), KV_W)
        v_all = v_all.reshape(batch * (seq + ctx_len), KV_W)
        merged = _mixer(u, q, k_all, v_all, wts, layer, batch, seq, tm)
        pre = _outproj(merged, w_out_b, xr, gt[layer], layer, lat_row, tm)
        if last:
            xr, = _ln_mod(pre, lat_row, ln=(ln_g[layer], ln_b[layer]))
        else:
            xr, u = _ln_mod(pre, lat_row, ln=(ln_g[layer], ln_b[layer]), mod=(sc[layer + 1], sh[layer + 1]))
    return xr.reshape(batch, seq, D_MODEL)
```

```python
import functools

import jax
import jax.numpy as jnp
from jax import lax
from jax.experimental import pallas as pl
from jax.experimental.pallas import tpu as pltpu

D_MODEL = 4096
DEPTH = 2
GRID_W = 64
HEAD_DIM = 128
N_Q_HEADS = 16
N_KV_HEADS = 4
GQA_GROUP = N_Q_HEADS // N_KV_HEADS
ATTN_W = N_Q_HEADS * HEAD_DIM
KV_W = N_KV_HEADS * HEAD_DIM
GROUP_W = GQA_GROUP * HEAD_DIM
AXIS_DIM = HEAD_DIM // 2
ROPE_THETA = 10000.0
CONV_W = 1024
GM_W = 1024
GM_GROUPS = 8
GM_GROUP_W = GM_W // GM_GROUPS
GM_CHUNK = 128
N_BRANCH = 3
EPS = 1e-6
ALPHA = (2.0 * DEPTH) ** 0.25

COL_Q = 0
COL_K = ATTN_W
COL_V = COL_K + KV_W
COL_AGATE = COL_V + KV_W
COL_CB = COL_AGATE + ATTN_W
COL_CC = COL_CB + CONV_W
COL_CH = COL_CC + CONV_W
COL_CGATE = COL_CH + CONV_W
COL_GU = COL_CGATE + CONV_W
COL_GV = COL_GU + GM_W
COL_GGATE = COL_GV + GM_W
COL_MGATE = COL_GGATE + GM_W
IN_COLS = COL_MGATE + N_BRANCH * D_MODEL

MOD_ROWS = 8
ONES_ROWS = 16
V7X_VMEM_LIMIT = 56 * 1024 * 1024

BF16 = jnp.bfloat16
F32 = jnp.float32


def _params(sem):
    return pltpu.CompilerParams(dimension_semantics=sem, vmem_limit_bytes=V7X_VMEM_LIMIT)


def _silu(x):
    return x * jax.nn.sigmoid(x)


def _modulation_body(c_ref, w_ref, b_ref, o_ref):
    s = _silu(c_ref[...]).astype(BF16)
    o_ref[...] = jnp.dot(s, w_ref[...].astype(BF16), preferred_element_type=F32) + b_ref[...]


def _modulation(cond, w_ada, b_ada, tn=512):
    n = 3 * D_MODEL
    return pl.pallas_call(
        _modulation_body,
        grid=(DEPTH, n // tn),
        in_specs=[
            pl.BlockSpec((MOD_ROWS, D_MODEL), lambda l, j: (0, 0)),
            pl.BlockSpec((None, D_MODEL, tn), lambda l, j: (l, 0, j)),
            pl.BlockSpec((None, 1, tn), lambda l, j: (l, 0, j)),
        ],
        out_specs=pl.BlockSpec((None, MOD_ROWS, tn), lambda l, j: (l, 0, j)),
        out_shape=jax.ShapeDtypeStruct((DEPTH, MOD_ROWS, n), F32),
        compiler_params=_params(("arbitrary", "arbitrary")),
        name="modulation",
    )(cond, w_ada, b_ada.reshape(DEPTH, 1, n))


def _ln_mod_body(*refs, do_ln, emit_x, emit_u):
    it = iter(refs)
    x_ref = next(it)
    g_ref = b_ref = sc_ref = sh_ref = None
    if do_ln:
        g_ref, b_ref = next(it), next(it)
    if emit_u:
        sc_ref, sh_ref = next(it), next(it)
    y = x_ref[...]
    if do_ln:
        mu = jnp.mean(y, axis=-1, keepdims=True)
        d = y - mu
        var = jnp.mean(d * d, axis=-1, keepdims=True)
        y = d * lax.rsqrt(var + EPS) * g_ref[...] + b_ref[...]
    if emit_x:
        next(it)[...] = y
    if emit_u:
        next(it)[...] = (y * (1.0 + sc_ref[...]) + sh_ref[...]).astype(BF16)


def _ln_mod(x, row_fn, ln=None, mod=None, emit_x=True, tr=256):
    m = x.shape[0]
    do_ln, emit_u = ln is not None, mod is not None
    row = pl.BlockSpec((tr, D_MODEL), lambda i: (i, 0))
    vec = pl.BlockSpec((1, D_MODEL), lambda i: (0, 0))
    mvec = pl.BlockSpec((None, 1, D_MODEL), lambda i: (row_fn(i * tr), 0, 0))
    ins, specs = [x], [row]
    if do_ln:
        ins += [ln[0].reshape(1, D_MODEL), ln[1].reshape(1, D_MODEL)]
        specs += [vec, vec]
    if emit_u:
        ins += [mod[0], mod[1]]
        specs += [mvec, mvec]
    out_shape, out_specs = [], []
    if emit_x:
        out_shape.append(jax.ShapeDtypeStruct((m, D_MODEL), F32))
        out_specs.append(row)
    if emit_u:
        out_shape.append(jax.ShapeDtypeStruct((m, D_MODEL), BF16))
        out_specs.append(row)
    return pl.pallas_call(
        functools.partial(_ln_mod_body, do_ln=do_ln, emit_x=emit_x, emit_u=emit_u),
        grid=(m // tr,),
        in_specs=specs,
        out_specs=out_specs,
        out_shape=out_shape,
        compiler_params=_params(("arbitrary",)),
        name="ln_mod",
    )(*ins)


def _proj_body(u_ref, *refs, n_w, n_extra, epilogue):
    w_refs = refs[:n_w]
    extra_refs = refs[n_w:n_w + n_extra]
    out_refs = refs[n_w + n_extra:]
    u = u_ref[...]
    accs = [jnp.dot(u, w[...], preferred_element_type=F32) for w in w_refs]
    epilogue(accs, extra_refs, out_refs)


def _proj(u, w, layer, cols, width, tm, tn, epilogue, n_out, extras=(), outs=None, tn_out=None, name="proj"):
    m, k = u.shape
    tn_out = tn if tn_out is None else tn_out
    stride = tn_out // tn
    in_specs = [pl.BlockSpec((tm, k), lambda i, j: (i, 0))]
    for c0 in cols:
        in_specs.append(pl.BlockSpec((None, k, tn), lambda i, j, cb=c0 // tn: (layer, 0, cb + j * stride)))
    in_specs += [spec for _, spec in extras]
    if outs is None:
        outs = [(jax.ShapeDtypeStruct((m, width), BF16),
                 pl.BlockSpec((tm, tn_out), lambda i, j: (i, j)))] * n_out
    return pl.pallas_call(
        functools.partial(_proj_body, n_w=len(cols), n_extra=len(extras), epilogue=epilogue),
        grid=(m // tm, width // tn_out),
        in_specs=in_specs,
        out_specs=[spec for _, spec in outs],
        out_shape=[shape for shape, _ in outs],
        compiler_params=_params(("arbitrary", "arbitrary")),
        name=name,
    )(u, *([w] * len(cols)), *[a for a, _ in extras])


def _epi_identity(accs, extras, outs):
    outs[0][...] = accs[0].astype(BF16)


def _epi_transpose(accs, extras, outs):
    outs[0][...] = accs[0].T.astype(BF16)


def _epi_silu(accs, extras, outs):
    outs[0][...] = _silu(accs[0]).astype(BF16)


def _epi_sigmoid(accs, extras, outs):
    outs[0][...] = jax.nn.sigmoid(accs[0]).astype(BF16)


def _epi_conv(accs, extras, outs):
    c_b, c_c, c_h, c_gate = accs
    outs[0][...] = (c_c * c_h).astype(BF16)
    outs[1][...] = (c_b * _silu(c_gate)).astype(BF16)


def _epi_gm(accs, extras, outs):
    g_u, g_v, g_gate = accs
    outs[0][...] = (g_u * _silu(g_gate)).astype(BF16)
    outs[1][...] = g_v.astype(BF16)


def _epi_norm_rope(accs, extras, outs, rope, scale):
    gain = extras[0][...]
    mix = extras[1][...]
    if rope:
        cos, sin = extras[2][...], extras[3][...]
    col = 0
    for acc in accs:
        for h in range(acc.shape[1] // HEAD_DIM):
            xh = acc[:, h * HEAD_DIM:(h + 1) * HEAD_DIM]
            y = xh * gain
            lhs = jnp.concatenate([y, xh * xh], axis=1).astype(BF16)
            r = jnp.dot(lhs, mix, preferred_element_type=F32)
            rstd = lax.rsqrt(r[:, HEAD_DIM:] * (1.0 / HEAD_DIM) + EPS)
            if rope:
                out = (y * cos + r[:, :HEAD_DIM] * sin) * rstd
            else:
                out = y * (rstd * scale)
            outs[0][:, col:col + HEAD_DIM] = out.astype(BF16)
            col += HEAD_DIM


def _swap_sumsq_matrix():
    j = jnp.arange(HEAD_DIM)
    half = AXIS_DIM // 2
    partner = jnp.where(j % AXIS_DIM < half, j + half, j - half)
    perm = (j[:, None] == partner[None, :]).astype(F32)
    zero = jnp.zeros((HEAD_DIM, HEAD_DIM), F32)
    top = jnp.concatenate([perm, zero], axis=1)
    bottom = jnp.concatenate([zero, jnp.ones((HEAD_DIM, HEAD_DIM), F32)], axis=1)
    return jnp.concatenate([top, bottom], axis=0).astype(BF16)


def _rope_tables(n):
    t = jnp.arange(n)
    pos = jnp.stack([t // GRID_W, t % GRID_W], axis=-1).astype(F32)
    freqs = ROPE_THETA ** (-jnp.arange(0, AXIS_DIM, 2, dtype=F32) / AXIS_DIM)
    ang = pos[:, :, None] * freqs
    cos, sin = jnp.cos(ang), jnp.sin(ang)
    cos = jnp.concatenate([cos, cos], axis=-1).reshape(n, HEAD_DIM)
    sin = jnp.concatenate([-sin, sin], axis=-1).reshape(n, HEAD_DIM)
    return cos, sin


def _attn_body(q_ref, k_ref, vt_ref, gate_ref, o_ref, *, tq, skv, tk):
    q = q_ref[...]
    qs = jnp.concatenate([q[:, g * HEAD_DIM:(g + 1) * HEAD_DIM] for g in range(GQA_GROUP)], axis=0)
    m = acc = None
    for start in range(0, skv, tk):
        size = min(tk, skv - start)
        st = lax.dot_general(k_ref[start:start + size, :], qs, (((1,), (1,)), ((), ())),
                             preferred_element_type=F32)
        m_chunk = jnp.max(st, axis=0, keepdims=True)
        m_new = m_chunk if m is None else jnp.maximum(m, m_chunk)
        p = jnp.exp2(st - m_new).astype(BF16)
        vt_ones = jnp.concatenate([vt_ref[:, start:start + size], jnp.ones((ONES_ROWS, size), BF16)], axis=0)
        pv = jnp.dot(vt_ones, p, preferred_element_type=F32)
        acc = pv if m is None else jnp.exp2(m - m_new) * acc + pv
        m = m_new
    o = (acc[:HEAD_DIM] / acc[HEAD_DIM:HEAD_DIM + 1]).T
    for g in range(GQA_GROUP):
        sl = slice(g * HEAD_DIM, (g + 1) * HEAD_DIM)
        o_ref[:, sl] = (o[g * tq:(g + 1) * tq] * gate_ref[:, sl].astype(F32)).astype(BF16)


def _attention(q, k, vt, gate, batch, tq, tk=512):
    sq = q.shape[0] // batch
    skv = k.shape[0] // batch
    nq = sq // tq
    qspec = pl.BlockSpec((tq, GROUP_W), lambda b, h, i: (b * nq + i, h))
    kspec = pl.BlockSpec((skv, HEAD_DIM), lambda b, h, i: (b, h))
    vspec = pl.BlockSpec((HEAD_DIM, skv), lambda b, h, i: (b * N_KV_HEADS + h, 0))
    return pl.pallas_call(
        functools.partial(_attn_body, tq=tq, skv=skv, tk=tk),
        grid=(batch, N_KV_HEADS, nq),
        in_specs=[qspec, kspec, vspec, qspec],
        out_specs=qspec,
        out_shape=jax.ShapeDtypeStruct(q.shape, BF16),
        compiler_params=_params(("arbitrary", "arbitrary", "arbitrary")),
        name="attention",
    )(q, k, vt, gate)


def _mix_body(z_ref, zp_ref, zn_ref, cg_ref, ug_ref, gv_ref, cw_ref, lg_ref, lb_ref, ws_ref, bias_ref,
              yc_ref, yg_ref, *, tr, tiles_per_seq):
    i = pl.program_id(0)
    z = z_ref[...].astype(F32)
    rows = lax.broadcasted_iota(jnp.int32, z.shape, 0)
    prev_row = jnp.where(i % tiles_per_seq == 0, 0.0, zp_ref[7:8, :].astype(F32))
    next_row = jnp.where((i + 1) % tiles_per_seq == 0, 0.0, zn_ref[0:1, :].astype(F32))
    z_prev = jnp.where(rows == 0, prev_row, pltpu.roll(z, 1, 0))
    z_next = jnp.where(rows == tr - 1, next_row, pltpu.roll(z, tr - 1, 0))
    conv = cw_ref[0:1, :] * z_prev + cw_ref[1:2, :] * z + cw_ref[2:3, :] * z_next
    yc_ref[...] = (cg_ref[...].astype(F32) * conv).astype(BF16)

    for c in range(tr // GM_CHUNK):
        rs = slice(c * GM_CHUNK, (c + 1) * GM_CHUNK)
        gv = gv_ref[rs, :].astype(F32)
        mu = jnp.mean(gv, axis=-1, keepdims=True)
        d = gv - mu
        var = jnp.mean(d * d, axis=-1, keepdims=True)
        vn = (d * lax.rsqrt(var + EPS) * lg_ref[...] + lb_ref[...]).astype(BF16)
        for g in range(GM_GROUPS):
            cs = slice(g * GM_GROUP_W, (g + 1) * GM_GROUP_W)
            s = jnp.dot(ws_ref[g], vn[:, cs], preferred_element_type=F32) + bias_ref[g]
            yg_ref[rs, cs] = (ug_ref[rs, cs].astype(F32) * s).astype(BF16)


def _mix(z, cgate, ug, gv, conv_w, ln_g, ln_b, ws, bias, seq, tr):
    m = z.shape[0]
    row = pl.BlockSpec((tr, CONV_W), lambda i: (i, 0))
    r8 = tr // 8
    prev8 = pl.BlockSpec((8, CONV_W), lambda i: (jnp.maximum(i * r8 - 1, 0), 0))
    next8 = pl.BlockSpec((8, CONV_W), lambda i: (jnp.minimum((i + 1) * r8, m // 8 - 1), 0))
    vec = pl.BlockSpec((1, GM_W), lambda i: (0, 0))
    full3 = pl.BlockSpec((GM_GROUPS, GM_CHUNK, GM_CHUNK), lambda i: (0, 0, 0))
    return pl.pallas_call(
        functools.partial(_mix_body, tr=tr, tiles_per_seq=seq // tr),
        grid=(m // tr,),
        in_specs=[row, prev8, next8, row, row, row,
                  pl.BlockSpec((3, CONV_W), lambda i: (0, 0)), vec, vec, full3, full3],
        out_specs=[row, row],
        out_shape=[jax.ShapeDtypeStruct((m, CONV_W), BF16), jax.ShapeDtypeStruct((m, GM_W), BF16)],
        compiler_params=_params(("arbitrary",)),
        name="conv_gmlp",
    )(z, z, z, cgate, ug, gv, conv_w, ln_g.reshape(1, GM_W), ln_b.reshape(1, GM_W), ws, bias)


def _merge_body(ya_ref, yc_ref, yg_ref, wa_ref, wc_ref, wg_ref, ga_ref, gc_ref, gg_ref, o_ref):
    def branch(gate_ref, y_ref, w_ref):
        return gate_ref[...].astype(F32) * jnp.dot(y_ref[...], w_ref[...].astype(BF16),
                                                   preferred_element_type=F32)

    acc = branch(ga_ref, ya_ref, wa_ref) + branch(gc_ref, yc_ref, wc_ref) + branch(gg_ref, yg_ref, wg_ref)
    o_ref[...] = acc.astype(BF16)


def _merge(ya, yc, yg, w_a, w_c, w_g, gates, layer, tm, tn=512):
    m = ya.shape[0]
    nj = D_MODEL // tn

    def lhs(width):
        return pl.BlockSpec((tm, width), lambda i, j: (i, 0))

    def wspec(width):
        return pl.BlockSpec((None, width, tn), lambda i, j: (layer, 0, j))

    def gspec(branch):
        return pl.BlockSpec((tm, tn), lambda i, j: (i, branch * nj + j))

    return pl.pallas_call(
        _merge_body,
        grid=(m // tm, nj),
        in_specs=[lhs(ATTN_W), lhs(CONV_W), lhs(GM_W), wspec(ATTN_W), wspec(CONV_W), wspec(GM_W),
                  gspec(0), gspec(1), gspec(2)],
        out_specs=pl.BlockSpec((tm, tn), lambda i, j: (i, j)),
        out_shape=jax.ShapeDtypeStruct((m, D_MODEL), BF16),
        compiler_params=_params(("arbitrary", "arbitrary")),
        name="merge",
    )(ya, yc, yg, w_a, w_c, w_g, gates, gates, gates)


def _outproj_body(m_ref, w_ref, x_ref, gt_ref, o_ref):
    acc = jnp.dot(m_ref[...], w_ref[...].astype(BF16), preferred_element_type=F32)
    o_ref[...] = ALPHA * x_ref[...] + gt_ref[...] * acc


def _outproj(merged, w_out, x, gt, layer, row_fn, tm, tn=512):
    m = merged.shape[0]
    return pl.pallas_call(
        _outproj_body,
        grid=(m // tm, D_MODEL // tn),
        in_specs=[
            pl.BlockSpec((tm, D_MODEL), lambda i, j: (i, 0)),
            pl.BlockSpec((None, D_MODEL, tn), lambda i, j: (layer, 0, j)),
            pl.BlockSpec((tm, tn), lambda i, j: (i, j)),
            pl.BlockSpec((None, 1, tn), lambda i, j: (row_fn(i * tm), 0, j)),
        ],
        out_specs=pl.BlockSpec((tm, tn), lambda i, j: (i, j)),
        out_shape=jax.ShapeDtypeStruct((m, D_MODEL), F32),
        compiler_params=_params(("arbitrary", "arbitrary")),
        name="outproj",
    )(merged, w_out, x, gt)


Q_SCALE = HEAD_DIM ** -0.5 * 1.4426950408889634


def _qkv(u, w_in, layer, q_norm, k_norm, rope, seq, tm, want_q=True):
    m = u.shape[0]
    tiles_per_seq = seq // tm
    half = 2 * HEAD_DIM
    gain = pl.BlockSpec((1, HEAD_DIM), lambda i, j: (0, 0))
    mix = pl.BlockSpec((2 * HEAD_DIM, 2 * HEAD_DIM), lambda i, j: (0, 0))
    tab = pl.BlockSpec((tm, HEAD_DIM), lambda i, j: (i % tiles_per_seq, 0))

    def extras(g, scale):
        ex = [(g.reshape(1, HEAD_DIM), gain), (_swap_sumsq_matrix(), mix)]
        if rope is not None:
            ex += [(rope[0] * scale, tab), (rope[1] * scale, tab)]
        return ex

    q = None
    if want_q:
        epi_q = functools.partial(_epi_norm_rope, rope=rope is not None, scale=Q_SCALE)
        q, = _proj(u, w_in, layer, [COL_Q, COL_Q + half], ATTN_W, tm, half, epi_q, 1,
                   extras(q_norm[layer], Q_SCALE), tn_out=2 * half, name="proj_q")
    epi_k = functools.partial(_epi_norm_rope, rope=rope is not None, scale=1.0)
    k, = _proj(u, w_in, layer, [COL_K, COL_K + half], KV_W, tm, half, epi_k, 1,
               extras(k_norm[layer], 1.0), tn_out=2 * half, name="proj_k")
    vt_out = (jax.ShapeDtypeStruct((m // seq * KV_W, seq), BF16),
              pl.BlockSpec((KV_W, tm), lambda i, j: (i // tiles_per_seq, i % tiles_per_seq)))
    vt, = _proj(u, w_in, layer, [COL_V], KV_W, tm, KV_W, _epi_transpose, 1, outs=[vt_out], name="proj_v")
    return q, k, vt


def _mixer(u, q, k_all, v_all, wts, layer, batch, seq, tm):
    w_in = wts["w_in"]
    agate, = _proj(u, w_in, layer, [COL_AGATE], ATTN_W, tm, 512, _epi_silu, 1, name="proj_agate")
    z, cgate = _proj(u, w_in, layer, [COL_CB, COL_CC, COL_CH, COL_CGATE], CONV_W, tm, 256, _epi_conv, 2,
                     name="proj_conv")
    ug, gv = _proj(u, w_in, layer, [COL_GU, COL_GV, COL_GGATE], GM_W, tm, 256, _epi_gm, 2, name="proj_gm")
    gates, = _proj(u, w_in, layer, [COL_MGATE], N_BRANCH * D_MODEL, tm, 512, _epi_sigmoid, 1,
                   name="proj_mgate")
    ya = _attention(q, k_all, v_all, agate, batch, min(seq, 512))
    yc, yg = _mix(z, cgate, ug, gv, wts["conv_w"][layer], wts["gm_ln_g"][layer], wts["gm_ln_b"][layer],
                  wts["gm_ws"][layer], wts["gm_bias"][layer], seq, min(tm, 512))
    return _merge(ya, yc, yg, wts["w_br_attn"], wts["w_br_conv"], wts["w_br_gm"], gates, layer, tm)


def kernel(x, c, ctx, c_ctx, w_ada, b_ada, w_in, q_norm, k_norm, conv_w, gm_ln_g, gm_ln_b, gm_ws, gm_b,
           w_br_attn, w_br_conv, w_br_gm, w_out, ln_g, ln_b):
    batch, seq, _ = x.shape
    ctx_len = ctx.shape[1]
    tm, tm_c = 1024, ctx_len

    wts = {
        "w_in": w_in.astype(BF16), "w_br_attn": w_br_attn, "w_br_conv": w_br_conv,
        "w_br_gm": w_br_gm, "conv_w": conv_w, "gm_ln_g": gm_ln_g, "gm_ln_b": gm_ln_b,
        "gm_ws": gm_ws.astype(BF16),
        "gm_bias": jnp.broadcast_to(gm_b[:, :, :, None], gm_b.shape + (GM_GROUP_W,)),
    }
    w_out_b = w_out

    cond = jnp.zeros((MOD_ROWS, D_MODEL), F32).at[:batch].set(c).at[batch].set(c_ctx)
    mod = _modulation(cond, w_ada, b_ada)
    mod = mod.reshape(DEPTH, MOD_ROWS, 1, 3, D_MODEL)
    sh, sc, gt = mod[:, :, :, 0], mod[:, :, :, 1], mod[:, :, :, 2]

    lat_row = lambda r: r // seq
    ctx_row = lambda r: batch

    rope = _rope_tables(seq)
    xr = x.reshape(batch * seq, D_MODEL)
    cr = ctx.reshape(batch * ctx_len, D_MODEL)

    u, = _ln_mod(xr, lat_row, mod=(sc[0], sh[0]), emit_x=False)
    u_c, = _ln_mod(cr, ctx_row, mod=(sc[0], sh[0]), emit_x=False)

    for layer in range(DEPTH):
        last = layer == DEPTH - 1
        q_c, k_c, vt_c = _qkv(u_c, wts["w_in"], layer, q_norm, k_norm, None, ctx_len, tm_c, want_q=not last)
        if not last:
            merged_c = _mixer(u_c, q_c, k_c, vt_c, wts, layer, batch, ctx_len, tm_c)
            pre_c = _outproj(merged_c, w_out_b, cr, gt[layer], layer, ctx_row, tm_c)
            cr, u_c = _ln_mod(pre_c, ctx_row, ln=(ln_g[layer], ln_b[layer]),
                              mod=(sc[layer + 1], sh[layer + 1]))
        q, k, vt = _qkv(u, wts["w_in"], layer, q_norm, k_norm, rope, seq, tm)
        k_all = jnp.concatenate([k.reshape(batch, seq, KV_W), k_c.reshape(batch, ctx_len, KV_W)], axis=1)
        k_all = k_all.reshape(batch * (seq + ctx_len), KV_W)
        vt_all = jnp.concatenate([vt, vt_c], axis=1)
        merged = _mixer(u, q, k_all, vt_all, wts, layer, batch, seq, tm)
        pre = _outproj(merged, w_out_b, xr, gt[layer], layer, lat_row, tm)
        if last:
            xr, = _ln_mod(pre, lat_row, ln=(ln_g[layer], ln_b[layer]))
        else:
            xr, u = _ln_mod(pre, lat_row, ln=(ln_g[layer], ln_b[layer]), mod=(sc[layer + 1], sh[layer + 1]))
    return xr.reshape(batch, seq, D_MODEL)
```

```python
import functools

import jax
import jax.numpy as jnp
from jax import lax
from jax.experimental import pallas as pl
from jax.experimental.pallas import tpu as pltpu

D_MODEL = 4096
DEPTH = 2
GRID_W = 64
HEAD_DIM = 128
N_Q_HEADS = 16
N_KV_HEADS = 4
GQA_GROUP = N_Q_HEADS // N_KV_HEADS
ATTN_W = N_Q_HEADS * HEAD_DIM
KV_W = N_KV_HEADS * HEAD_DIM
GROUP_W = GQA_GROUP * HEAD_DIM
AXIS_DIM = HEAD_DIM // 2
ROPE_THETA = 10000.0
CONV_W = 1024
GM_W = 1024
GM_GROUPS = 8
GM_GROUP_W = GM_W // GM_GROUPS
GM_CHUNK = 128
N_BRANCH = 3
EPS = 1e-6
ALPHA = (2.0 * DEPTH) ** 0.25

COL_Q = 0
COL_K = ATTN_W
COL_V = COL_K + KV_W
COL_AGATE = COL_V + KV_W
COL_CB = COL_AGATE + ATTN_W
COL_CC = COL_CB + CONV_W
COL_CH = COL_CC + CONV_W
COL_CGATE = COL_CH + CONV_W
COL_GU = COL_CGATE + CONV_W
COL_GV = COL_GU + GM_W
COL_GGATE = COL_GV + GM_W
COL_MGATE = COL_GGATE + GM_W
IN_COLS = COL_MGATE + N_BRANCH * D_MODEL

MOD_ROWS = 8
ONES_ROWS = 16
SUB_M = 512
WIDE_TN = 1024
V7X_VMEM_LIMIT = 56 * 1024 * 1024

BF16 = jnp.bfloat16
F32 = jnp.float32


def _params(sem, flags=None):
    return pltpu.CompilerParams(dimension_semantics=sem, vmem_limit_bytes=V7X_VMEM_LIMIT, flags=flags)


def _silu(x):
    return x * jax.nn.sigmoid(x)


def _modulation_body(c_ref, w_ref, b_ref, o_ref):
    s = _silu(c_ref[...]).astype(BF16)
    o_ref[...] = jnp.dot(s, w_ref[...].astype(BF16), preferred_element_type=F32) + b_ref[...]


def _modulation(cond, w_ada, b_ada, tn=512):
    n = 3 * D_MODEL
    return pl.pallas_call(
        _modulation_body,
        grid=(DEPTH, n // tn),
        in_specs=[
            pl.BlockSpec((MOD_ROWS, D_MODEL), lambda l, j: (0, 0)),
            pl.BlockSpec((None, D_MODEL, tn), lambda l, j: (l, 0, j)),
            pl.BlockSpec((None, 1, tn), lambda l, j: (l, 0, j)),
        ],
        out_specs=pl.BlockSpec((None, MOD_ROWS, tn), lambda l, j: (l, 0, j)),
        out_shape=jax.ShapeDtypeStruct((DEPTH, MOD_ROWS, n), F32),
        compiler_params=_params(("arbitrary", "arbitrary")),
        name="modulation",
    )(cond, w_ada, b_ada.reshape(DEPTH, 1, n))


def _ln_mod_body(*refs, do_ln, emit_x, emit_u):
    it = iter(refs)
    x_ref = next(it)
    g_ref = b_ref = sc_ref = sh_ref = None
    if do_ln:
        g_ref, b_ref = next(it), next(it)
    if emit_u:
        sc_ref, sh_ref = next(it), next(it)
    y = x_ref[...]
    if do_ln:
        mu = jnp.mean(y, axis=-1, keepdims=True)
        d = y - mu
        var = jnp.mean(d * d, axis=-1, keepdims=True)
        y = d * lax.rsqrt(var + EPS) * g_ref[...] + b_ref[...]
    if emit_x:
        next(it)[...] = y
    if emit_u:
        next(it)[...] = (y * (1.0 + sc_ref[...]) + sh_ref[...]).astype(BF16)


def _ln_mod(x, row_fn, ln=None, mod=None, emit_x=True, tr=256):
    m = x.shape[0]
    do_ln, emit_u = ln is not None, mod is not None
    row = pl.BlockSpec((tr, D_MODEL), lambda i: (i, 0))
    vec = pl.BlockSpec((1, D_MODEL), lambda i: (0, 0))
    mvec = pl.BlockSpec((None, 1, D_MODEL), lambda i: (row_fn(i * tr), 0, 0))
    ins, specs = [x], [row]
    if do_ln:
        ins += [ln[0].reshape(1, D_MODEL), ln[1].reshape(1, D_MODEL)]
        specs += [vec, vec]
    if emit_u:
        ins += [mod[0], mod[1]]
        specs += [mvec, mvec]
    out_shape, out_specs = [], []
    if emit_x:
        out_shape.append(jax.ShapeDtypeStruct((m, D_MODEL), F32))
        out_specs.append(row)
    if emit_u:
        out_shape.append(jax.ShapeDtypeStruct((m, D_MODEL), BF16))
        out_specs.append(row)
    return pl.pallas_call(
        functools.partial(_ln_mod_body, do_ln=do_ln, emit_x=emit_x, emit_u=emit_u),
        grid=(m // tr,),
        in_specs=specs,
        out_specs=out_specs,
        out_shape=out_shape,
        compiler_params=_params(("arbitrary",)),
        name="ln_mod",
    )(*ins)


def _proj_body(u_ref, *refs, n_w, n_extra, epilogue):
    w_refs = refs[:n_w]
    extra_refs = refs[n_w:n_w + n_extra]
    out_refs = refs[n_w + n_extra:]
    for rows in _row_blocks(u_ref.shape[0]):
        u = u_ref[rows, :]
        accs = [jnp.dot(u, w[...], preferred_element_type=F32) for w in w_refs]
        epilogue(accs, extra_refs, out_refs, rows)


def _proj(u, w, layer, cols, width, tm, tn, epilogue, n_out, extras=(), outs=None, tn_out=None, name="proj"):
    m, k = u.shape
    tn_out = tn if tn_out is None else tn_out
    stride = tn_out // tn
    in_specs = [pl.BlockSpec((tm, k), lambda i, j: (i, 0))]
    for c0 in cols:
        in_specs.append(pl.BlockSpec((None, k, tn), lambda i, j, cb=c0 // tn: (layer, 0, cb + j * stride)))
    in_specs += [spec for _, spec in extras]
    if outs is None:
        outs = [(jax.ShapeDtypeStruct((m, width), BF16),
                 pl.BlockSpec((tm, tn_out), lambda i, j: (i, j)))] * n_out
    return pl.pallas_call(
        functools.partial(_proj_body, n_w=len(cols), n_extra=len(extras), epilogue=epilogue),
        grid=(m // tm, width // tn_out),
        in_specs=in_specs,
        out_specs=[spec for _, spec in outs],
        out_shape=[shape for shape, _ in outs],
        compiler_params=_params(("arbitrary", "arbitrary")),
        name=name,
    )(u, *([w] * len(cols)), *[a for a, _ in extras])


def _row_blocks(tm):
    step = min(tm, SUB_M)
    return [slice(r, r + step) for r in range(0, tm, step)]


def _epi_transpose(accs, extras, outs, rows):
    outs[0][:, rows] = accs[0].T.astype(BF16)


def _epi_silu(accs, extras, outs, rows):
    outs[0][rows, :] = _silu(accs[0]).astype(BF16)


def _epi_sigmoid(accs, extras, outs, rows):
    outs[0][rows, :] = jax.nn.sigmoid(accs[0]).astype(BF16)


def _epi_conv(accs, extras, outs, rows):
    c_b, c_c, c_h, c_gate = accs
    outs[0][rows, :] = (c_c * c_h).astype(BF16)
    outs[1][rows, :] = (c_b * _silu(c_gate)).astype(BF16)


def _epi_gm(accs, extras, outs, rows):
    g_u, g_v, g_gate = accs
    outs[0][rows, :] = (g_u * _silu(g_gate)).astype(BF16)
    outs[1][rows, :] = g_v.astype(BF16)


def _epi_norm_rope(accs, extras, outs, rows, rope, scale):
    gain = extras[0][...]
    mix = extras[1][...]
    if rope:
        cos, sin = extras[2][rows, :], extras[3][rows, :]
    col = 0
    for acc in accs:
        for h in range(acc.shape[1] // HEAD_DIM):
            xh = acc[:, h * HEAD_DIM:(h + 1) * HEAD_DIM]
            y = xh * gain
            lhs = jnp.concatenate([y, xh * xh], axis=1).astype(BF16)
            r = jnp.dot(lhs, mix, preferred_element_type=F32)
            rstd = lax.rsqrt(r[:, HEAD_DIM:] * (1.0 / HEAD_DIM) + EPS)
            if rope:
                out = (y * cos + r[:, :HEAD_DIM] * sin) * rstd
            else:
                out = y * (rstd * scale)
            outs[0][rows, col:col + HEAD_DIM] = out.astype(BF16)
            col += HEAD_DIM


def _swap_sumsq_matrix():
    j = jnp.arange(HEAD_DIM)
    half = AXIS_DIM // 2
    partner = jnp.where(j % AXIS_DIM < half, j + half, j - half)
    perm = (j[:, None] == partner[None, :]).astype(F32)
    zero = jnp.zeros((HEAD_DIM, HEAD_DIM), F32)
    top = jnp.concatenate([perm, zero], axis=1)
    bottom = jnp.concatenate([zero, jnp.ones((HEAD_DIM, HEAD_DIM), F32)], axis=1)
    return jnp.concatenate([top, bottom], axis=0).astype(BF16)


def _rope_tables(n):
    t = jnp.arange(n)
    pos = jnp.stack([t // GRID_W, t % GRID_W], axis=-1).astype(F32)
    freqs = ROPE_THETA ** (-jnp.arange(0, AXIS_DIM, 2, dtype=F32) / AXIS_DIM)
    ang = pos[:, :, None] * freqs
    cos, sin = jnp.cos(ang), jnp.sin(ang)
    cos = jnp.concatenate([cos, cos], axis=-1).reshape(n, HEAD_DIM)
    sin = jnp.concatenate([-sin, sin], axis=-1).reshape(n, HEAD_DIM)
    return cos, sin


def _attn_body(q_ref, k_ref, vt_ref, gate_ref, *rest, tq, skv, tk):
    if len(rest) == 3:
        w_ref, o_ref, wb_ref = rest
        wb_ref[...] = w_ref[...].astype(BF16)
    else:
        o_ref, = rest
    qs = jnp.concatenate([q_ref[:, g * HEAD_DIM:(g + 1) * HEAD_DIM] for g in range(GQA_GROUP)], axis=0)
    m = acc = None
    for start in range(0, skv, tk):
        size = min(tk, skv - start)
        st = lax.dot_general(k_ref[start:start + size, :], qs, (((1,), (1,)), ((), ())),
                             preferred_element_type=F32)
        m_chunk = jnp.max(st, axis=0, keepdims=True)
        m_new = m_chunk if m is None else jnp.maximum(m, m_chunk)
        p = jnp.exp2(st - m_new).astype(BF16)
        vt_ones = jnp.concatenate([vt_ref[:, start:start + size], jnp.ones((ONES_ROWS, size), BF16)], axis=0)
        pv = jnp.dot(vt_ones, p, preferred_element_type=F32)
        acc = pv if m is None else jnp.exp2(m - m_new) * acc + pv
        m = m_new
    o = (acc[:HEAD_DIM] / acc[HEAD_DIM:HEAD_DIM + 1]).T
    for g in range(GQA_GROUP):
        sl = slice(g * HEAD_DIM, (g + 1) * HEAD_DIM)
        o_ref[:, sl] = (o[g * tq:(g + 1) * tq] * gate_ref[:, sl].astype(F32)).astype(BF16)


def _attention(q, k, vt, gate, batch, tq, tk=512, cast=None):
    sq = q.shape[0] // batch
    skv = k.shape[0] // batch
    nq = sq // tq
    qspec = pl.BlockSpec((tq, GROUP_W), lambda b, h, i: (b * nq + i, h))
    kspec = pl.BlockSpec((skv, HEAD_DIM), lambda b, h, i: (b, h))
    vspec = pl.BlockSpec((HEAD_DIM, skv), lambda b, h, i: (b * N_KV_HEADS + h, 0))
    ins, in_specs = [q, k, vt, gate], [qspec, kspec, vspec, qspec]
    out_shape, out_specs = [jax.ShapeDtypeStruct(q.shape, BF16)], [qspec]
    if cast is not None:
        w, layer = cast
        _, kdim, ndim = w.shape
        slab = ndim // (batch * N_KV_HEADS * nq)
        step = lambda b, h, i: (b * N_KV_HEADS + h) * nq + i
        ins.append(w)
        in_specs.append(pl.BlockSpec((None, kdim, slab), lambda b, h, i: (layer, 0, step(b, h, i))))
        out_shape.append(jax.ShapeDtypeStruct((1, kdim, ndim), BF16))
        out_specs.append(pl.BlockSpec((None, kdim, slab), lambda b, h, i: (0, 0, step(b, h, i))))
    return pl.pallas_call(
        functools.partial(_attn_body, tq=tq, skv=skv, tk=tk),
        grid=(batch, N_KV_HEADS, nq),
        in_specs=in_specs,
        out_specs=out_specs,
        out_shape=out_shape,
        compiler_params=_params(("arbitrary", "arbitrary", "arbitrary")),
        name="attention",
    )(*ins)


def _mix_body(z_ref, zp_ref, zn_ref, cg_ref, ug_ref, gv_ref, cw_ref, lg_ref, lb_ref, ws_ref, bias_ref,
              yc_ref, yg_ref, *, tr, tiles_per_seq):
    i = pl.program_id(0)
    z = z_ref[...].astype(F32)
    rows = lax.broadcasted_iota(jnp.int32, z.shape, 0)
    prev_row = jnp.where(i % tiles_per_seq == 0, 0.0, zp_ref[7:8, :].astype(F32))
    next_row = jnp.where((i + 1) % tiles_per_seq == 0, 0.0, zn_ref[0:1, :].astype(F32))
    z_prev = jnp.where(rows == 0, prev_row, pltpu.roll(z, 1, 0))
    z_next = jnp.where(rows == tr - 1, next_row, pltpu.roll(z, tr - 1, 0))
    conv = cw_ref[0:1, :] * z_prev + cw_ref[1:2, :] * z + cw_ref[2:3, :] * z_next
    yc_ref[...] = (cg_ref[...].astype(F32) * conv).astype(BF16)

    for c in range(tr // GM_CHUNK):
        rs = slice(c * GM_CHUNK, (c + 1) * GM_CHUNK)
        gv = gv_ref[rs, :].astype(F32)
        mu = jnp.mean(gv, axis=-1, keepdims=True)
        d = gv - mu
        var = jnp.mean(d * d, axis=-1, keepdims=True)
        vn = (d * lax.rsqrt(var + EPS) * lg_ref[...] + lb_ref[...]).astype(BF16)
        for g in range(GM_GROUPS):
            cs = slice(g * GM_GROUP_W, (g + 1) * GM_GROUP_W)
            s = jnp.dot(ws_ref[g], vn[:, cs], preferred_element_type=F32) + bias_ref[g]
            yg_ref[rs, cs] = (ug_ref[rs, cs].astype(F32) * s).astype(BF16)


def _mix(z, cgate, ug, gv, conv_w, ln_g, ln_b, ws, bias, seq, tr):
    m = z.shape[0]
    row = pl.BlockSpec((tr, CONV_W), lambda i: (i, 0))
    r8 = tr // 8
    prev8 = pl.BlockSpec((8, CONV_W), lambda i: (jnp.maximum(i * r8 - 1, 0), 0))
    next8 = pl.BlockSpec((8, CONV_W), lambda i: (jnp.minimum((i + 1) * r8, m // 8 - 1), 0))
    vec = pl.BlockSpec((1, GM_W), lambda i: (0, 0))
    full3 = pl.BlockSpec((GM_GROUPS, GM_CHUNK, GM_CHUNK), lambda i: (0, 0, 0))
    return pl.pallas_call(
        functools.partial(_mix_body, tr=tr, tiles_per_seq=seq // tr),
        grid=(m // tr,),
        in_specs=[row, prev8, next8, row, row, row,
                  pl.BlockSpec((3, CONV_W), lambda i: (0, 0)), vec, vec, full3, full3],
        out_specs=[row, row],
        out_shape=[jax.ShapeDtypeStruct((m, CONV_W), BF16), jax.ShapeDtypeStruct((m, GM_W), BF16)],
        compiler_params=_params(("arbitrary",)),
        name="conv_gmlp",
    )(z, z, z, cgate, ug, gv, conv_w, ln_g.reshape(1, GM_W), ln_b.reshape(1, GM_W), ws, bias)


def _merge_body(ya_ref, yc_ref, yg_ref, wa_ref, wc_ref, wg_ref, ga_ref, gc_ref, gg_ref, o_ref):
    weights = [w_ref[...].astype(BF16) for w_ref in (wa_ref, wc_ref, wg_ref)]
    for rows in _row_blocks(o_ref.shape[0]):
        acc = None
        for gate_ref, y_ref, w in zip((ga_ref, gc_ref, gg_ref), (ya_ref, yc_ref, yg_ref), weights):
            term = gate_ref[rows, :].astype(F32) * jnp.dot(y_ref[rows, :], w, preferred_element_type=F32)
            acc = term if acc is None else acc + term
        o_ref[rows, :] = acc.astype(BF16)


def _merge(ya, yc, yg, w_a, w_c, w_g, gates, layer, tm, tn=512):
    m = ya.shape[0]
    nj = D_MODEL // tn

    def lhs(width):
        return pl.BlockSpec((tm, width), lambda i, j: (i, 0))

    def wspec(width):
        return pl.BlockSpec((None, width, tn), lambda i, j: (layer, 0, j))

    def gspec(branch):
        return pl.BlockSpec((tm, tn), lambda i, j: (i, branch * nj + j))

    return pl.pallas_call(
        _merge_body,
        grid=(m // tm, nj),
        in_specs=[lhs(ATTN_W), lhs(CONV_W), lhs(GM_W), wspec(ATTN_W), wspec(CONV_W), wspec(GM_W),
                  gspec(0), gspec(1), gspec(2)],
        out_specs=pl.BlockSpec((tm, tn), lambda i, j: (i, j)),
        out_shape=jax.ShapeDtypeStruct((m, D_MODEL), BF16),
        compiler_params=_params(("arbitrary", "arbitrary")),
        name="merge",
    )(ya, yc, yg, w_a, w_c, w_g, gates, gates, gates)


def _outproj_body(m_ref, w_ref, x_ref, gt_ref, o_ref):
    w = w_ref[...].astype(BF16)
    for rows in _row_blocks(o_ref.shape[0]):
        acc = jnp.dot(m_ref[rows, :], w, preferred_element_type=F32)
        o_ref[rows, :] = ALPHA * x_ref[rows, :] + gt_ref[...] * acc


def _outproj(merged, w_out, x, gt, layer, row_fn, tm, tn=512):
    m = merged.shape[0]
    return pl.pallas_call(
        _outproj_body,
        grid=(m // tm, D_MODEL // tn),
        in_specs=[
            pl.BlockSpec((tm, D_MODEL), lambda i, j: (i, 0)),
            pl.BlockSpec((None, D_MODEL, tn), lambda i, j: (layer, 0, j)),
            pl.BlockSpec((tm, tn), lambda i, j: (i, j)),
            pl.BlockSpec((None, 1, tn), lambda i, j: (row_fn(i * tm), 0, j)),
        ],
        out_specs=pl.BlockSpec((tm, tn), lambda i, j: (i, j)),
        out_shape=jax.ShapeDtypeStruct((m, D_MODEL), F32),
        compiler_params=_params(("arbitrary", "arbitrary")),
        name="outproj",
    )(merged, w_out, x, gt)


Q_SCALE = HEAD_DIM ** -0.5 * 1.4426950408889634


def _qkv(u, w_in, layer, q_norm, k_norm, rope, seq, tm, want_q=True):
    w_arr, w_idx = w_in
    m = u.shape[0]
    tiles_per_seq = seq // tm
    half = 2 * HEAD_DIM
    gain = pl.BlockSpec((1, HEAD_DIM), lambda i, j: (0, 0))
    mix = pl.BlockSpec((2 * HEAD_DIM, 2 * HEAD_DIM), lambda i, j: (0, 0))
    tab = pl.BlockSpec((tm, HEAD_DIM), lambda i, j: (i % tiles_per_seq, 0))

    def extras(g, scale):
        ex = [(g.reshape(1, HEAD_DIM), gain), (_swap_sumsq_matrix(), mix)]
        if rope is not None:
            ex += [(rope[0] * scale, tab), (rope[1] * scale, tab)]
        return ex

    q = None
    if want_q:
        epi_q = functools.partial(_epi_norm_rope, rope=rope is not None, scale=Q_SCALE)
        q, = _proj(u, w_arr, w_idx, [COL_Q + s * half for s in range(WIDE_TN // half)], ATTN_W, tm, half,
                   epi_q, 1, extras(q_norm[layer], Q_SCALE), tn_out=WIDE_TN, name="proj_q")
    epi_k = functools.partial(_epi_norm_rope, rope=rope is not None, scale=1.0)
    k, = _proj(u, w_arr, w_idx, [COL_K, COL_K + half], KV_W, tm, half, epi_k, 1,
               extras(k_norm[layer], 1.0), tn_out=2 * half, name="proj_k")
    vt_out = (jax.ShapeDtypeStruct((m // seq * KV_W, seq), BF16),
              pl.BlockSpec((KV_W, tm), lambda i, j: (i // tiles_per_seq, i % tiles_per_seq)))
    vt, = _proj(u, w_arr, w_idx, [COL_V], KV_W, tm, KV_W, _epi_transpose, 1, outs=[vt_out], name="proj_v")
    return q, k, vt


def _mixer(u, q, k_all, v_all, wts, w_in, layer, batch, seq, tm, cast=None):
    w_arr, w_idx = w_in
    agate, = _proj(u, w_arr, w_idx, [COL_AGATE], ATTN_W, tm, WIDE_TN, _epi_silu, 1, name="proj_agate")
    z, cgate = _proj(u, w_arr, w_idx, [COL_CB, COL_CC, COL_CH, COL_CGATE], CONV_W, tm, 256, _epi_conv, 2,
                     name="proj_conv")
    ug, gv = _proj(u, w_arr, w_idx, [COL_GU, COL_GV, COL_GGATE], GM_W, tm, 256, _epi_gm, 2, name="proj_gm")
    gates, = _proj(u, w_arr, w_idx, [COL_MGATE], N_BRANCH * D_MODEL, tm, WIDE_TN, _epi_sigmoid, 1,
                   name="proj_mgate")
    ya, *w_cast = _attention(q, k_all, v_all, agate, batch, min(seq, 512), cast=cast)
    yc, yg = _mix(z, cgate, ug, gv, wts["conv_w"][layer], wts["gm_ln_g"][layer], wts["gm_ln_b"][layer],
                  wts["gm_ws"][layer], wts["gm_bias"][layer], seq, min(seq, 512))
    merged = _merge(ya, yc, yg, wts["w_br_attn"], wts["w_br_conv"], wts["w_br_gm"], gates, layer, tm)
    return merged, (w_cast[0] if w_cast else None)


def kernel(x, c, ctx, c_ctx, w_ada, b_ada, w_in, q_norm, k_norm, conv_w, gm_ln_g, gm_ln_b, gm_ws, gm_b,
           w_br_attn, w_br_conv, w_br_gm, w_out, ln_g, ln_b):
    batch, seq, _ = x.shape
    ctx_len = ctx.shape[1]
    tm, tm_c = 1024, ctx_len

    w_in_l = (w_in[:1].astype(BF16), 0)
    wts = {
        "w_br_attn": w_br_attn, "w_br_conv": w_br_conv,
        "w_br_gm": w_br_gm, "conv_w": conv_w, "gm_ln_g": gm_ln_g, "gm_ln_b": gm_ln_b,
        "gm_ws": gm_ws.astype(BF16),
        "gm_bias": jnp.broadcast_to(gm_b[:, :, :, None], gm_b.shape + (GM_GROUP_W,)),
    }
    w_out_b = w_out

    cond = jnp.zeros((MOD_ROWS, D_MODEL), F32).at[:batch].set(c).at[batch].set(c_ctx)
    mod = _modulation(cond, w_ada, b_ada)
    mod = mod.reshape(DEPTH, MOD_ROWS, 1, 3, D_MODEL)
    sh, sc, gt = mod[:, :, :, 0], mod[:, :, :, 1], mod[:, :, :, 2]

    lat_row = lambda r: r // seq
    ctx_row = lambda r: batch

    rope = _rope_tables(seq)
    xr = x.reshape(batch * seq, D_MODEL)
    cr = ctx.reshape(batch * ctx_len, D_MODEL)

    u, = _ln_mod(xr, lat_row, mod=(sc[0], sh[0]), emit_x=False)
    u_c, = _ln_mod(cr, ctx_row, mod=(sc[0], sh[0]), emit_x=False)

    for layer in range(DEPTH):
        last = layer == DEPTH - 1
        q_c, k_c, vt_c = _qkv(u_c, w_in_l, layer, q_norm, k_norm, None, ctx_len, tm_c, want_q=not last)
        if not last:
            merged_c, _ = _mixer(u_c, q_c, k_c, vt_c, wts, w_in_l, layer, batch, ctx_len, batch * ctx_len)
            pre_c = _outproj(merged_c, w_out_b, cr, gt[layer], layer, ctx_row, batch * ctx_len)
            cr, u_c = _ln_mod(pre_c, ctx_row, ln=(ln_g[layer], ln_b[layer]),
                              mod=(sc[layer + 1], sh[layer + 1]))
        q, k, vt = _qkv(u, w_in_l, layer, q_norm, k_norm, rope, seq, tm)
        k_all = jnp.concatenate([k.reshape(batch, seq, KV_W), k_c.reshape(batch, ctx_len, KV_W)], axis=1)
        k_all = k_all.reshape(batch * (seq + ctx_len), KV_W)
        vt_all = jnp.concatenate([vt, vt_c], axis=1)
        merged, w_next = _mixer(u, q, k_all, vt_all, wts, w_in_l, layer, batch, seq, tm,
                                cast=None if last else (w_in, layer + 1))
        w_in_l = (w_next, 0)
        pre = _outproj(merged, w_out_b, xr, gt[layer], layer, lat_row, tm)
        if last:
            xr, = _ln_mod(pre, lat_row, ln=(ln_g[layer], ln_b[layer]))
        else:
            xr, u = _ln_mod(pre, lat_row, ln=(ln_g[layer], ln_b[layer]), mod=(sc[layer + 1], sh[layer + 1]))
    return xr.reshape(batch, seq, D_MODEL)
```

```python
import functools

import jax
import jax.numpy as jnp
from jax import lax
from jax.experimental import pallas as pl
from jax.experimental.pallas import tpu as pltpu

D_MODEL = 4096
DEPTH = 2
GRID_W = 64
HEAD_DIM = 128
N_Q_HEADS = 16
N_KV_HEADS = 4
GQA_GROUP = N_Q_HEADS // N_KV_HEADS
ATTN_W = N_Q_HEADS * HEAD_DIM
KV_W = N_KV_HEADS * HEAD_DIM
GROUP_W = GQA_GROUP * HEAD_DIM
AXIS_DIM = HEAD_DIM // 2
ROPE_THETA = 10000.0
CONV_W = 1024
GM_W = 1024
GM_GROUPS = 8
GM_GROUP_W = GM_W // GM_GROUPS
GM_CHUNK = 128
N_BRANCH = 3
EPS = 1e-6
ALPHA = (2.0 * DEPTH) ** 0.25

COL_Q = 0
COL_K = ATTN_W
COL_V = COL_K + KV_W
COL_AGATE = COL_V + KV_W
COL_CB = COL_AGATE + ATTN_W
COL_CC = COL_CB + CONV_W
COL_CH = COL_CC + CONV_W
COL_CGATE = COL_CH + CONV_W
COL_GU = COL_CGATE + CONV_W
COL_GV = COL_GU + GM_W
COL_GGATE = COL_GV + GM_W
COL_MGATE = COL_GGATE + GM_W
IN_COLS = COL_MGATE + N_BRANCH * D_MODEL

MOD_ROWS = 8
ONES_ROWS = 16
SUB_M = 512
WIDE_TN = 1024
V7X_VMEM_LIMIT = 56 * 1024 * 1024

BF16 = jnp.bfloat16
F32 = jnp.float32


def _params(sem, flags=None):
    return pltpu.CompilerParams(dimension_semantics=sem, vmem_limit_bytes=V7X_VMEM_LIMIT, flags=flags)


def _silu(x):
    return x * jax.nn.sigmoid(x)


def _modulation_body(c_ref, w_ref, b_ref, o_ref):
    s = _silu(c_ref[...]).astype(BF16)
    o_ref[...] = jnp.dot(s, w_ref[...].astype(BF16), preferred_element_type=F32) + b_ref[...]


def _modulation(cond, w_ada, b_ada, tn=512):
    n = 3 * D_MODEL
    return pl.pallas_call(
        _modulation_body,
        grid=(DEPTH, n // tn),
        in_specs=[
            pl.BlockSpec((MOD_ROWS, D_MODEL), lambda l, j: (0, 0)),
            pl.BlockSpec((None, D_MODEL, tn), lambda l, j: (l, 0, j)),
            pl.BlockSpec((None, 1, tn), lambda l, j: (l, 0, j)),
        ],
        out_specs=pl.BlockSpec((None, MOD_ROWS, tn), lambda l, j: (l, 0, j)),
        out_shape=jax.ShapeDtypeStruct((DEPTH, MOD_ROWS, n), F32),
        compiler_params=_params(("arbitrary", "arbitrary")),
        name="modulation",
    )(cond, w_ada, b_ada.reshape(DEPTH, 1, n))


def _ln_mod_body(*refs, do_ln, emit_x, emit_u):
    it = iter(refs)
    x_ref = next(it)
    g_ref = b_ref = sc_ref = sh_ref = None
    if do_ln:
        g_ref, b_ref = next(it), next(it)
    if emit_u:
        sc_ref, sh_ref = next(it), next(it)
    y = x_ref[...]
    if do_ln:
        mu = jnp.mean(y, axis=-1, keepdims=True)
        d = y - mu
        var = jnp.mean(d * d, axis=-1, keepdims=True)
        y = d * lax.rsqrt(var + EPS) * g_ref[...] + b_ref[...]
    if emit_x:
        next(it)[...] = y
    if emit_u:
        next(it)[...] = (y * (1.0 + sc_ref[...]) + sh_ref[...]).astype(BF16)


def _ln_mod(x, row_fn, ln=None, mod=None, emit_x=True, tr=256):
    m = x.shape[0]
    do_ln, emit_u = ln is not None, mod is not None
    row = pl.BlockSpec((tr, D_MODEL), lambda i: (i, 0))
    vec = pl.BlockSpec((1, D_MODEL), lambda i: (0, 0))
    mvec = pl.BlockSpec((None, 1, D_MODEL), lambda i: (row_fn(i * tr), 0, 0))
    ins, specs = [x], [row]
    if do_ln:
        ins += [ln[0].reshape(1, D_MODEL), ln[1].reshape(1, D_MODEL)]
        specs += [vec, vec]
    if emit_u:
        ins += [mod[0], mod[1]]
        specs += [mvec, mvec]
    out_shape, out_specs = [], []
    if emit_x:
        out_shape.append(jax.ShapeDtypeStruct((m, D_MODEL), F32))
        out_specs.append(row)
    if emit_u:
        out_shape.append(jax.ShapeDtypeStruct((m, D_MODEL), BF16))
        out_specs.append(row)
    return pl.pallas_call(
        functools.partial(_ln_mod_body, do_ln=do_ln, emit_x=emit_x, emit_u=emit_u),
        grid=(m // tr,),
        in_specs=specs,
        out_specs=out_specs,
        out_shape=out_shape,
        compiler_params=_params(("arbitrary",)),
        name="ln_mod",
    )(*ins)


def _proj_body(u_ref, *refs, n_w, n_extra, epilogue, emit_w):
    w_refs = refs[:n_w]
    extra_refs = refs[n_w:n_w + n_extra]
    out_refs = refs[n_w + n_extra:]
    weights = [w[...].astype(BF16) for w in w_refs]
    if emit_w:
        out_refs, wb_ref = out_refs[:-1], out_refs[-1]
        wb_ref[...] = weights[0]
    for rows in _row_blocks(u_ref.shape[0]):
        u = u_ref[rows, :]
        accs = [jnp.dot(u, w, preferred_element_type=F32) for w in weights]
        epilogue(accs, extra_refs, out_refs, rows)


def _proj(u, w, layer, cols, width, tm, tn, epilogue, n_out, extras=(), outs=None, tn_out=None,
          emit_w=False, name="proj"):
    m, k = u.shape
    tn_out = tn if tn_out is None else tn_out
    stride = tn_out // tn
    in_specs = [pl.BlockSpec((tm, k), lambda i, j: (i, 0))]
    for c0 in cols:
        in_specs.append(pl.BlockSpec((None, k, tn), lambda i, j, cb=c0 // tn: (layer, 0, cb + j * stride)))
    in_specs += [spec for _, spec in extras]
    if outs is None:
        outs = [(jax.ShapeDtypeStruct((m, width), BF16),
                 pl.BlockSpec((tm, tn_out), lambda i, j: (i, j)))] * n_out
    if emit_w:
        assert m == tm and len(cols) == 1 and stride == 1
        outs = list(outs) + [(jax.ShapeDtypeStruct((1, k, width), BF16),
                              pl.BlockSpec((None, k, tn), lambda i, j: (0, 0, j)))]
    return pl.pallas_call(
        functools.partial(_proj_body, n_w=len(cols), n_extra=len(extras), epilogue=epilogue, emit_w=emit_w),
        grid=(m // tm, width // tn_out),
        in_specs=in_specs,
        out_specs=[spec for _, spec in outs],
        out_shape=[shape for shape, _ in outs],
        compiler_params=_params(("arbitrary", "arbitrary")),
        name=name,
    )(u, *([w] * len(cols)), *[a for a, _ in extras])


def _row_blocks(tm):
    step = min(tm, SUB_M)
    return [slice(r, r + step) for r in range(0, tm, step)]


def _epi_transpose(accs, extras, outs, rows):
    outs[0][:, rows] = accs[0].T.astype(BF16)


def _epi_silu(accs, extras, outs, rows):
    outs[0][rows, :] = _silu(accs[0]).astype(BF16)


def _epi_sigmoid(accs, extras, outs, rows):
    outs[0][rows, :] = jax.nn.sigmoid(accs[0]).astype(BF16)


def _epi_conv(accs, extras, outs, rows):
    c_b, c_c, c_h, c_gate = accs
    outs[0][rows, :] = (c_c * c_h).astype(BF16)
    outs[1][rows, :] = (c_b * _silu(c_gate)).astype(BF16)


def _epi_gm(accs, extras, outs, rows):
    g_u, g_v, g_gate = accs
    outs[0][rows, :] = (g_u * _silu(g_gate)).astype(BF16)
    outs[1][rows, :] = g_v.astype(BF16)


def _epi_norm_rope(accs, extras, outs, rows, rope, scale):
    gain = extras[0][...]
    mix = extras[1][...]
    if rope:
        cos, sin = extras[2][rows, :], extras[3][rows, :]
    col = 0
    for acc in accs:
        for h in range(acc.shape[1] // HEAD_DIM):
            xh = acc[:, h * HEAD_DIM:(h + 1) * HEAD_DIM]
            y = xh * gain
            lhs = jnp.concatenate([y, xh * xh], axis=1).astype(BF16)
            r = jnp.dot(lhs, mix, preferred_element_type=F32)
            rstd = lax.rsqrt(r[:, HEAD_DIM:] * (1.0 / HEAD_DIM) + EPS)
            if rope:
                out = (y * cos + r[:, :HEAD_DIM] * sin) * rstd
            else:
                out = y * (rstd * scale)
            outs[0][rows, col:col + HEAD_DIM] = out.astype(BF16)
            col += HEAD_DIM


def _swap_sumsq_matrix():
    j = jnp.arange(HEAD_DIM)
    half = AXIS_DIM // 2
    partner = jnp.where(j % AXIS_DIM < half, j + half, j - half)
    perm = (j[:, None] == partner[None, :]).astype(F32)
    zero = jnp.zeros((HEAD_DIM, HEAD_DIM), F32)
    top = jnp.concatenate([perm, zero], axis=1)
    bottom = jnp.concatenate([zero, jnp.ones((HEAD_DIM, HEAD_DIM), F32)], axis=1)
    return jnp.concatenate([top, bottom], axis=0).astype(BF16)


def _rope_tables(n):
    t = jnp.arange(n)
    pos = jnp.stack([t // GRID_W, t % GRID_W], axis=-1).astype(F32)
    freqs = ROPE_THETA ** (-jnp.arange(0, AXIS_DIM, 2, dtype=F32) / AXIS_DIM)
    ang = pos[:, :, None] * freqs
    cos, sin = jnp.cos(ang), jnp.sin(ang)
    cos = jnp.concatenate([cos, cos], axis=-1).reshape(n, HEAD_DIM)
    sin = jnp.concatenate([-sin, sin], axis=-1).reshape(n, HEAD_DIM)
    return cos, sin


def _attn_body(q_ref, k_ref, vt_ref, gate_ref, *rest, tq, skv, tk):
    if len(rest) == 3:
        w_ref, o_ref, wb_ref = rest
        wb_ref[...] = w_ref[...].astype(BF16)
    else:
        o_ref, = rest
    qs = jnp.concatenate([q_ref[:, g * HEAD_DIM:(g + 1) * HEAD_DIM] for g in range(GQA_GROUP)], axis=0)
    chunks = [(start, min(tk, skv - start)) for start in range(0, skv, tk)]

    def scores(start, size):
        return lax.dot_general(k_ref[start:start + size, :], qs, (((1,), (1,)), ((), ())),
                               preferred_element_type=F32)

    m = acc = None
    st_next = scores(*chunks[0])
    for n, (start, size) in enumerate(chunks):
        st, st_next = st_next, (scores(*chunks[n + 1]) if n + 1 < len(chunks) else None)
        m_chunk = jnp.max(st, axis=0, keepdims=True)
        m_new = m_chunk if m is None else jnp.maximum(m, m_chunk)
        p = jnp.exp2(st - m_new).astype(BF16)
        vt_ones = jnp.concatenate([vt_ref[:, start:start + size], jnp.ones((ONES_ROWS, size), BF16)], axis=0)
        pv = jnp.dot(vt_ones, p, preferred_element_type=F32)
        acc = pv if m is None else jnp.exp2(m - m_new) * acc + pv
        m = m_new
    o = (acc[:HEAD_DIM] / acc[HEAD_DIM:HEAD_DIM + 1]).T
    for g in range(GQA_GROUP):
        sl = slice(g * HEAD_DIM, (g + 1) * HEAD_DIM)
        o_ref[:, sl] = (o[g * tq:(g + 1) * tq] * gate_ref[:, sl].astype(F32)).astype(BF16)


def _attention(q, k, vt, gate, batch, tq, tk=256, cast=None):
    sq = q.shape[0] // batch
    skv = k.shape[0] // batch
    nq = sq // tq
    qspec = pl.BlockSpec((tq, GROUP_W), lambda b, h, i: (b * nq + i, h))
    kspec = pl.BlockSpec((skv, HEAD_DIM), lambda b, h, i: (b, h))
    vspec = pl.BlockSpec((HEAD_DIM, skv), lambda b, h, i: (b * N_KV_HEADS + h, 0))
    ins, in_specs = [q, k, vt, gate], [qspec, kspec, vspec, qspec]
    out_shape, out_specs = [jax.ShapeDtypeStruct(q.shape, BF16)], [qspec]
    if cast is not None:
        w, layer = cast
        _, kdim, ndim = w.shape
        slab = ndim // (batch * N_KV_HEADS * nq)
        step = lambda b, h, i: (b * N_KV_HEADS + h) * nq + i
        ins.append(w)
        in_specs.append(pl.BlockSpec((None, kdim, slab), lambda b, h, i: (layer, 0, step(b, h, i))))
        out_shape.append(jax.ShapeDtypeStruct((1, kdim, ndim), BF16))
        out_specs.append(pl.BlockSpec((None, kdim, slab), lambda b, h, i: (0, 0, step(b, h, i))))
    return pl.pallas_call(
        functools.partial(_attn_body, tq=tq, skv=skv, tk=tk),
        grid=(batch, N_KV_HEADS, nq),
        in_specs=in_specs,
        out_specs=out_specs,
        out_shape=out_shape,
        compiler_params=_params(("arbitrary", "arbitrary", "arbitrary")),
        name="attention",
    )(*ins)


def _mix_body(z_ref, zp_ref, zn_ref, cg_ref, ug_ref, gv_ref, cw_ref, lg_ref, lb_ref, ws_ref, bias_ref,
              yc_ref, yg_ref, *, tr, tiles_per_seq):
    i = pl.program_id(0)
    z = z_ref[...].astype(F32)
    rows = lax.broadcasted_iota(jnp.int32, z.shape, 0)
    prev_row = jnp.where(i % tiles_per_seq == 0, 0.0, zp_ref[7:8, :].astype(F32))
    next_row = jnp.where((i + 1) % tiles_per_seq == 0, 0.0, zn_ref[0:1, :].astype(F32))
    z_prev = jnp.where(rows == 0, prev_row, pltpu.roll(z, 1, 0))
    z_next = jnp.where(rows == tr - 1, next_row, pltpu.roll(z, tr - 1, 0))
    conv = cw_ref[0:1, :] * z_prev + cw_ref[1:2, :] * z + cw_ref[2:3, :] * z_next
    yc_ref[...] = (cg_ref[...].astype(F32) * conv).astype(BF16)

    for c in range(tr // GM_CHUNK):
        rs = slice(c * GM_CHUNK, (c + 1) * GM_CHUNK)
        gv = gv_ref[rs, :].astype(F32)
        mu = jnp.mean(gv, axis=-1, keepdims=True)
        d = gv - mu
        var = jnp.mean(d * d, axis=-1, keepdims=True)
        vn = (d * lax.rsqrt(var + EPS) * lg_ref[...] + lb_ref[...]).astype(BF16)
        for g in range(GM_GROUPS):
            cs = slice(g * GM_GROUP_W, (g + 1) * GM_GROUP_W)
            s = jnp.dot(ws_ref[g], vn[:, cs], preferred_element_type=F32) + bias_ref[g]
            yg_ref[rs, cs] = (ug_ref[rs, cs].astype(F32) * s).astype(BF16)


def _mix(z, cgate, ug, gv, conv_w, ln_g, ln_b, ws, bias, seq, tr):
    m = z.shape[0]
    row = pl.BlockSpec((tr, CONV_W), lambda i: (i, 0))
    r8 = tr // 8
    prev8 = pl.BlockSpec((8, CONV_W), lambda i: (jnp.maximum(i * r8 - 1, 0), 0))
    next8 = pl.BlockSpec((8, CONV_W), lambda i: (jnp.minimum((i + 1) * r8, m // 8 - 1), 0))
    vec = pl.BlockSpec((1, GM_W), lambda i: (0, 0))
    full3 = pl.BlockSpec((GM_GROUPS, GM_CHUNK, GM_CHUNK), lambda i: (0, 0, 0))
    return pl.pallas_call(
        functools.partial(_mix_body, tr=tr, tiles_per_seq=seq // tr),
        grid=(m // tr,),
        in_specs=[row, prev8, next8, row, row, row,
                  pl.BlockSpec((3, CONV_W), lambda i: (0, 0)), vec, vec, full3, full3],
        out_specs=[row, row],
        out_shape=[jax.ShapeDtypeStruct((m, CONV_W), BF16), jax.ShapeDtypeStruct((m, GM_W), BF16)],
        compiler_params=_params(("arbitrary",)),
        name="conv_gmlp",
    )(z, z, z, cgate, ug, gv, conv_w, ln_g.reshape(1, GM_W), ln_b.reshape(1, GM_W), ws, bias)


def _merge_body(ya_ref, yc_ref, yg_ref, wa_ref, wc_ref, wg_ref, ga_ref, gc_ref, gg_ref, o_ref):
    weights = [w_ref[...].astype(BF16) for w_ref in (wa_ref, wc_ref, wg_ref)]
    for rows in _row_blocks(o_ref.shape[0]):
        acc = None
        for gate_ref, y_ref, w in zip((ga_ref, gc_ref, gg_ref), (ya_ref, yc_ref, yg_ref), weights):
            term = gate_ref[rows, :].astype(F32) * jnp.dot(y_ref[rows, :], w, preferred_element_type=F32)
            acc = term if acc is None else acc + term
        o_ref[rows, :] = acc.astype(BF16)


def _merge(ya, yc, yg, w_a, w_c, w_g, gates, layer, tm, tn=512):
    m = ya.shape[0]
    nj = D_MODEL // tn

    def lhs(width):
        return pl.BlockSpec((tm, width), lambda i, j: (i, 0))

    def wspec(width):
        return pl.BlockSpec((None, width, tn), lambda i, j: (layer, 0, j))

    def gspec(branch):
        return pl.BlockSpec((tm, tn), lambda i, j: (i, branch * nj + j))

    return pl.pallas_call(
        _merge_body,
        grid=(m // tm, nj),
        in_specs=[lhs(ATTN_W), lhs(CONV_W), lhs(GM_W), wspec(ATTN_W), wspec(CONV_W), wspec(GM_W),
                  gspec(0), gspec(1), gspec(2)],
        out_specs=pl.BlockSpec((tm, tn), lambda i, j: (i, j)),
        out_shape=jax.ShapeDtypeStruct((m, D_MODEL), BF16),
        compiler_params=_params(("arbitrary", "arbitrary")),
        name="merge",
    )(ya, yc, yg, w_a, w_c, w_g, gates, gates, gates)


def _outproj_body(m_ref, w_ref, x_ref, gt_ref, o_ref):
    w = w_ref[...].astype(BF16)
    for rows in _row_blocks(o_ref.shape[0]):
        acc = jnp.dot(m_ref[rows, :], w, preferred_element_type=F32)
        o_ref[rows, :] = ALPHA * x_ref[rows, :] + gt_ref[...] * acc


def _outproj(merged, w_out, x, gt, layer, row_fn, tm, tn=512):
    m = merged.shape[0]
    return pl.pallas_call(
        _outproj_body,
        grid=(m // tm, D_MODEL // tn),
        in_specs=[
            pl.BlockSpec((tm, D_MODEL), lambda i, j: (i, 0)),
            pl.BlockSpec((None, D_MODEL, tn), lambda i, j: (layer, 0, j)),
            pl.BlockSpec((tm, tn), lambda i, j: (i, j)),
            pl.BlockSpec((None, 1, tn), lambda i, j: (row_fn(i * tm), 0, j)),
        ],
        out_specs=pl.BlockSpec((tm, tn), lambda i, j: (i, j)),
        out_shape=jax.ShapeDtypeStruct((m, D_MODEL), F32),
        compiler_params=_params(("arbitrary", "arbitrary")),
        name="outproj",
    )(merged, w_out, x, gt)


Q_SCALE = HEAD_DIM ** -0.5 * 1.4426950408889634


def _qkv(u, w_in, layer, q_norm, k_norm, rope, seq, tm, want_q=True):
    w_arr, w_idx = w_in
    m = u.shape[0]
    tiles_per_seq = seq // tm
    half = 2 * HEAD_DIM
    gain = pl.BlockSpec((1, HEAD_DIM), lambda i, j: (0, 0))
    mix = pl.BlockSpec((2 * HEAD_DIM, 2 * HEAD_DIM), lambda i, j: (0, 0))
    tab = pl.BlockSpec((tm, HEAD_DIM), lambda i, j: (i % tiles_per_seq, 0))

    def extras(g, scale):
        ex = [(g.reshape(1, HEAD_DIM), gain), (_swap_sumsq_matrix(), mix)]
        if rope is not None:
            ex += [(rope[0] * scale, tab), (rope[1] * scale, tab)]
        return ex

    q = None
    if want_q:
        epi_q = functools.partial(_epi_norm_rope, rope=rope is not None, scale=Q_SCALE)
        q, = _proj(u, w_arr, w_idx, [COL_Q + s * half for s in range(WIDE_TN // half)], ATTN_W, tm, half,
                   epi_q, 1, extras(q_norm[layer], Q_SCALE), tn_out=WIDE_TN, name="proj_q")
    epi_k = functools.partial(_epi_norm_rope, rope=rope is not None, scale=1.0)
    k, = _proj(u, w_arr, w_idx, [COL_K, COL_K + half], KV_W, tm, half, epi_k, 1,
               extras(k_norm[layer], 1.0), tn_out=2 * half, name="proj_k")
    vt_out = (jax.ShapeDtypeStruct((m // seq * KV_W, seq), BF16),
              pl.BlockSpec((KV_W, tm), lambda i, j: (i // tiles_per_seq, i % tiles_per_seq)))
    vt, = _proj(u, w_arr, w_idx, [COL_V], KV_W, tm, KV_W, _epi_transpose, 1, outs=[vt_out], name="proj_v")
    return q, k, vt


def _mixer(u, q, k_all, v_all, wts, w_in, w_mgate, layer, batch, seq, tm, cast=None):
    w_arr, w_idx = w_in
    agate, = _proj(u, w_arr, w_idx, [COL_AGATE], ATTN_W, tm, WIDE_TN, _epi_silu, 1, name="proj_agate")
    z, cgate = _proj(u, w_arr, w_idx, [COL_CB, COL_CC, COL_CH, COL_CGATE], CONV_W, tm, 256, _epi_conv, 2,
                     name="proj_conv")
    ug, gv = _proj(u, w_arr, w_idx, [COL_GU, COL_GV, COL_GGATE], GM_W, tm, 256, _epi_gm, 2, name="proj_gm")
    w_copy = None
    if len(w_mgate) == 2:
        gates, w_copy = _proj(u, w_mgate[0], w_mgate[1], [COL_MGATE], N_BRANCH * D_MODEL, tm, WIDE_TN // 2,
                              _epi_sigmoid, 1, emit_w=True, name="proj_mgate")
    else:
        gates, = _proj(u, w_mgate[0], w_mgate[1], [w_mgate[2]], N_BRANCH * D_MODEL, tm, WIDE_TN,
                       _epi_sigmoid, 1, name="proj_mgate")
    ya, *w_cast = _attention(q, k_all, v_all, agate, batch, min(seq, 512), cast=cast)
    yc, yg = _mix(z, cgate, ug, gv, wts["conv_w"][layer], wts["gm_ln_g"][layer], wts["gm_ln_b"][layer],
                  wts["gm_ws"][layer], wts["gm_bias"][layer], seq, min(seq, 512))
    merged = _merge(ya, yc, yg, wts["w_br_attn"], wts["w_br_conv"], wts["w_br_gm"], gates, layer, tm)
    return merged, (w_cast[0] if w_cast else None), w_copy


def kernel(x, c, ctx, c_ctx, w_ada, b_ada, w_in, q_norm, k_norm, conv_w, gm_ln_g, gm_ln_b, gm_ws, gm_b,
           w_br_attn, w_br_conv, w_br_gm, w_out, ln_g, ln_b):
    batch, seq, _ = x.shape
    ctx_len = ctx.shape[1]
    tm, tm_c = 1024, ctx_len

    w_in_l = (w_in[:1, :, :COL_MGATE].astype(BF16), 0)
    w_mgate = (w_in, 0)
    wts = {
        "w_br_attn": w_br_attn, "w_br_conv": w_br_conv,
        "w_br_gm": w_br_gm, "conv_w": conv_w, "gm_ln_g": gm_ln_g, "gm_ln_b": gm_ln_b,
        "gm_ws": gm_ws.astype(BF16),
        "gm_bias": jnp.broadcast_to(gm_b[:, :, :, None], gm_b.shape + (GM_GROUP_W,)),
    }
    w_out_b = w_out

    cond = jnp.zeros((MOD_ROWS, D_MODEL), F32).at[:batch].set(c).at[batch].set(c_ctx)
    mod = _modulation(cond, w_ada, b_ada)
    mod = mod.reshape(DEPTH, MOD_ROWS, 1, 3, D_MODEL)
    sh, sc, gt = mod[:, :, :, 0], mod[:, :, :, 1], mod[:, :, :, 2]

    lat_row = lambda r: r // seq
    ctx_row = lambda r: batch

    rope = _rope_tables(seq)
    xr = x.reshape(batch * seq, D_MODEL)
    cr = ctx.reshape(batch * ctx_len, D_MODEL)

    u, = _ln_mod(xr, lat_row, mod=(sc[0], sh[0]), emit_x=False)
    u_c, = _ln_mod(cr, ctx_row, mod=(sc[0], sh[0]), emit_x=False)

    for layer in range(DEPTH):
        last = layer == DEPTH - 1
        q_c, k_c, vt_c = _qkv(u_c, w_in_l, layer, q_norm, k_norm, None, ctx_len, tm_c, want_q=not last)
        if not last:
            merged_c, _, w_copy = _mixer(u_c, q_c, k_c, vt_c, wts, w_in_l, w_mgate, layer, batch, ctx_len,
                                         batch * ctx_len)
            if w_copy is not None:
                w_mgate = (w_copy, 0, 0)
            pre_c = _outproj(merged_c, w_out_b, cr, gt[layer], layer, ctx_row, batch * ctx_len)
            cr, u_c = _ln_mod(pre_c, ctx_row, ln=(ln_g[layer], ln_b[layer]),
                              mod=(sc[layer + 1], sh[layer + 1]))
        q, k, vt = _qkv(u, w_in_l, layer, q_norm, k_norm, rope, seq, tm)
        k_all = jnp.concatenate([k.reshape(batch, seq, KV_W), k_c.reshape(batch, ctx_len, KV_W)], axis=1)
        k_all = k_all.reshape(batch * (seq + ctx_len), KV_W)
        vt_all = jnp.concatenate([vt, vt_c], axis=1)
        merged, w_next, _ = _mixer(u, q, k_all, vt_all, wts, w_in_l, w_mgate, layer, batch, seq, tm,
                                   cast=None if last else (w_in, layer + 1))
        w_in_l, w_mgate = (w_next, 0), (w_next, 0, COL_MGATE)
        pre = _outproj(merged, w_out_b, xr, gt[layer], layer, lat_row, tm)
        if last:
            xr, = _ln_mod(pre, lat_row, ln=(ln_g[layer], ln_b[layer]))
        else:
            xr, u = _ln_mod(pre, lat_row, ln=(ln_g[layer], ln_b[layer]), mod=(sc[layer + 1], sh[layer + 1]))
    return xr.reshape(batch, seq, D_MODEL)
```

```python
import functools

import jax
import jax.numpy as jnp
from jax import lax
from jax.experimental import pallas as pl
from jax.experimental.pallas import tpu as pltpu

D_MODEL = 4096
DEPTH = 2
GRID_W = 64
HEAD_DIM = 128
N_Q_HEADS = 16
N_KV_HEADS = 4
GQA_GROUP = N_Q_HEADS // N_KV_HEADS
ATTN_W = N_Q_HEADS * HEAD_DIM
KV_W = N_KV_HEADS * HEAD_DIM
GROUP_W = GQA_GROUP * HEAD_DIM
AXIS_DIM = HEAD_DIM // 2
ROPE_THETA = 10000.0
CONV_W = 1024
GM_W = 1024
GM_GROUPS = 8
GM_GROUP_W = GM_W // GM_GROUPS
GM_CHUNK = 128
N_BRANCH = 3
EPS = 1e-6
ALPHA = (2.0 * DEPTH) ** 0.25

COL_Q = 0
COL_K = ATTN_W
COL_V = COL_K + KV_W
COL_AGATE = COL_V + KV_W
COL_CB = COL_AGATE + ATTN_W
COL_CC = COL_CB + CONV_W
COL_CH = COL_CC + CONV_W
COL_CGATE = COL_CH + CONV_W
COL_GU = COL_CGATE + CONV_W
COL_GV = COL_GU + GM_W
COL_GGATE = COL_GV + GM_W
COL_MGATE = COL_GGATE + GM_W
IN_COLS = COL_MGATE + N_BRANCH * D_MODEL

MOD_ROWS = 8
ONES_ROWS = 16
SUB_M = 512
WIDE_TN = 1024
V7X_VMEM_LIMIT = 56 * 1024 * 1024

BF16 = jnp.bfloat16
F32 = jnp.float32


def _params(sem, flags=None):
    return pltpu.CompilerParams(dimension_semantics=sem, vmem_limit_bytes=V7X_VMEM_LIMIT, flags=flags)


def _silu(x):
    return x * jax.nn.sigmoid(x)


def _modulation_body(c_ref, w_ref, b_ref, o_ref):
    s = _silu(c_ref[...]).astype(BF16)
    o_ref[...] = jnp.dot(s, w_ref[...].astype(BF16), preferred_element_type=F32) + b_ref[...]


def _modulation(cond, w_ada, b_ada, tn=512):
    n = 3 * D_MODEL
    return pl.pallas_call(
        _modulation_body,
        grid=(DEPTH, n // tn),
        in_specs=[
            pl.BlockSpec((MOD_ROWS, D_MODEL), lambda l, j: (0, 0)),
            pl.BlockSpec((None, D_MODEL, tn), lambda l, j: (l, 0, j)),
            pl.BlockSpec((None, 1, tn), lambda l, j: (l, 0, j)),
        ],
        out_specs=pl.BlockSpec((None, MOD_ROWS, tn), lambda l, j: (l, 0, j)),
        out_shape=jax.ShapeDtypeStruct((DEPTH, MOD_ROWS, n), F32),
        compiler_params=_params(("arbitrary", "arbitrary")),
        name="modulation",
    )(cond, w_ada, b_ada.reshape(DEPTH, 1, n))


def _ln_mod_body(*refs, do_ln, emit_x, emit_u):
    it = iter(refs)
    x_ref = next(it)
    g_ref = b_ref = sc_ref = sh_ref = None
    if do_ln:
        g_ref, b_ref = next(it), next(it)
    if emit_u:
        sc_ref, sh_ref = next(it), next(it)
    y = x_ref[...]
    if do_ln:
        mu = jnp.mean(y, axis=-1, keepdims=True)
        d = y - mu
        var = jnp.mean(d * d, axis=-1, keepdims=True)
        y = d * lax.rsqrt(var + EPS) * g_ref[...] + b_ref[...]
    if emit_x:
        next(it)[...] = y
    if emit_u:
        next(it)[...] = (y * (1.0 + sc_ref[...]) + sh_ref[...]).astype(BF16)


def _ln_mod(x, row_fn, ln=None, mod=None, emit_x=True, tr=256):
    m = x.shape[0]
    do_ln, emit_u = ln is not None, mod is not None
    row = pl.BlockSpec((tr, D_MODEL), lambda i: (i, 0))
    vec = pl.BlockSpec((1, D_MODEL), lambda i: (0, 0))
    mvec = pl.BlockSpec((None, 1, D_MODEL), lambda i: (row_fn(i * tr), 0, 0))
    ins, specs = [x], [row]
    if do_ln:
        ins += [ln[0].reshape(1, D_MODEL), ln[1].reshape(1, D_MODEL)]
        specs += [vec, vec]
    if emit_u:
        ins += [mod[0], mod[1]]
        specs += [mvec, mvec]
    out_shape, out_specs = [], []
    if emit_x:
        out_shape.append(jax.ShapeDtypeStruct((m, D_MODEL), F32))
        out_specs.append(row)
    if emit_u:
        out_shape.append(jax.ShapeDtypeStruct((m, D_MODEL), BF16))
        out_specs.append(row)
    return pl.pallas_call(
        functools.partial(_ln_mod_body, do_ln=do_ln, emit_x=emit_x, emit_u=emit_u),
        grid=(m // tr,),
        in_specs=specs,
        out_specs=out_specs,
        out_shape=out_shape,
        compiler_params=_params(("arbitrary",)),
        name="ln_mod",
    )(*ins)


def _proj_body(u_ref, *refs, n_w, n_extra, n_side, epilogue, emit_w):
    w_refs = refs[:n_w]
    extra_refs = refs[n_w:n_w + n_extra]
    side_refs = refs[n_w + n_extra:n_w + n_extra + n_side]
    out_refs = refs[n_w + n_extra + n_side:]
    if n_side:
        out_refs, cast_refs = out_refs[:-n_side], out_refs[-n_side:]
        for src, dst in zip(side_refs, cast_refs):
            dst[...] = src[...].astype(BF16)
    weights = [w[...].astype(BF16) for w in w_refs]
    if emit_w:
        out_refs, wb_ref = out_refs[:-1], out_refs[-1]
        wb_ref[...] = weights[0]
    for rows in _row_blocks(u_ref.shape[0]):
        u = u_ref[rows, :]
        accs = [jnp.dot(u, w, preferred_element_type=F32) for w in weights]
        epilogue(accs, extra_refs, out_refs, rows)


def _proj(u, w, layer, cols, width, tm, tn, epilogue, n_out, extras=(), outs=None, tn_out=None,
          emit_w=False, side_cast=(), name="proj"):
    m, k = u.shape
    tn_out = tn if tn_out is None else tn_out
    stride = tn_out // tn
    ni, nj = m // tm, width // tn_out
    in_specs = [pl.BlockSpec((tm, k), lambda i, j: (i, 0))]
    for c0 in cols:
        in_specs.append(pl.BlockSpec((None, k, tn), lambda i, j, cb=c0 // tn: (layer, 0, cb + j * stride)))
    in_specs += [spec for _, spec in extras]
    if outs is None:
        outs = [(jax.ShapeDtypeStruct((m, width), BF16),
                 pl.BlockSpec((tm, tn_out), lambda i, j: (i, j)))] * n_out
    outs = list(outs)
    if emit_w:
        assert m == tm and len(cols) == 1 and stride == 1
        outs.append((jax.ShapeDtypeStruct((1, k, width), BF16),
                     pl.BlockSpec((None, k, tn), lambda i, j: (0, 0, j))))
    for arr, idx in side_cast:
        _, r, c = arr.shape
        slab = c // (ni * nj)
        in_specs.append(pl.BlockSpec((None, r, slab), lambda i, j, idx=idx: (idx, 0, i * nj + j)))
        outs.append((jax.ShapeDtypeStruct((1, r, c), BF16),
                     pl.BlockSpec((None, r, slab), lambda i, j: (0, 0, i * nj + j))))
    return pl.pallas_call(
        functools.partial(_proj_body, n_w=len(cols), n_extra=len(extras), n_side=len(side_cast),
                          epilogue=epilogue, emit_w=emit_w),
        grid=(ni, nj),
        in_specs=in_specs,
        out_specs=[spec for _, spec in outs],
        out_shape=[shape for shape, _ in outs],
        compiler_params=_params(("arbitrary", "arbitrary")),
        name=name,
    )(u, *([w] * len(cols)), *[a for a, _ in extras], *[a for a, _ in side_cast])


def _row_blocks(tm):
    step = min(tm, SUB_M)
    return [slice(r, r + step) for r in range(0, tm, step)]


def _epi_transpose(accs, extras, outs, rows):
    outs[0][:, rows] = accs[0].T.astype(BF16)


def _epi_silu(accs, extras, outs, rows):
    outs[0][rows, :] = _silu(accs[0]).astype(BF16)


def _epi_sigmoid(accs, extras, outs, rows):
    outs[0][rows, :] = jax.nn.sigmoid(accs[0]).astype(BF16)


def _epi_conv(accs, extras, outs, rows):
    c_b, c_c, c_h, c_gate = accs
    outs[0][rows, :] = (c_c * c_h).astype(BF16)
    outs[1][rows, :] = (c_b * _silu(c_gate)).astype(BF16)


def _epi_gm(accs, extras, outs, rows):
    g_u, g_v, g_gate = accs
    outs[0][rows, :] = (g_u * _silu(g_gate)).astype(BF16)
    outs[1][rows, :] = g_v.astype(BF16)


def _epi_norm_rope(accs, extras, outs, rows, rope, scale):
    gain = extras[0][...]
    mix = extras[1][...]
    if rope:
        cos, sin = extras[2][rows, :], extras[3][rows, :]
    col = 0
    for acc in accs:
        for h in range(acc.shape[1] // HEAD_DIM):
            xh = acc[:, h * HEAD_DIM:(h + 1) * HEAD_DIM]
            y = xh * gain
            lhs = jnp.concatenate([y, xh * xh], axis=1).astype(BF16)
            r = jnp.dot(lhs, mix, preferred_element_type=F32)
            rstd = lax.rsqrt(r[:, HEAD_DIM:] * (1.0 / HEAD_DIM) + EPS)
            if rope:
                out = (y * cos + r[:, :HEAD_DIM] * sin) * rstd
            else:
                out = y * (rstd * scale)
            outs[0][rows, col:col + HEAD_DIM] = out.astype(BF16)
            col += HEAD_DIM


def _swap_sumsq_matrix():
    j = jnp.arange(HEAD_DIM)
    half = AXIS_DIM // 2
    partner = jnp.where(j % AXIS_DIM < half, j + half, j - half)
    perm = (j[:, None] == partner[None, :]).astype(F32)
    zero = jnp.zeros((HEAD_DIM, HEAD_DIM), F32)
    top = jnp.concatenate([perm, zero], axis=1)
    bottom = jnp.concatenate([zero, jnp.ones((HEAD_DIM, HEAD_DIM), F32)], axis=1)
    return jnp.concatenate([top, bottom], axis=0).astype(BF16)


def _rope_tables(n):
    t = jnp.arange(n)
    pos = jnp.stack([t // GRID_W, t % GRID_W], axis=-1).astype(F32)
    freqs = ROPE_THETA ** (-jnp.arange(0, AXIS_DIM, 2, dtype=F32) / AXIS_DIM)
    ang = pos[:, :, None] * freqs
    cos, sin = jnp.cos(ang), jnp.sin(ang)
    cos = jnp.concatenate([cos, cos], axis=-1).reshape(n, HEAD_DIM)
    sin = jnp.concatenate([-sin, sin], axis=-1).reshape(n, HEAD_DIM)
    return cos, sin


def _attn_body(q_ref, k_ref, vt_ref, gate_ref, *rest, tq, skv, tk):
    if len(rest) == 3:
        w_ref, o_ref, wb_ref = rest
        wb_ref[...] = w_ref[...].astype(BF16)
    else:
        o_ref, = rest
    qs = jnp.concatenate([q_ref[:, g * HEAD_DIM:(g + 1) * HEAD_DIM] for g in range(GQA_GROUP)], axis=0)
    chunks = [(start, min(tk, skv - start)) for start in range(0, skv, tk)]

    def scores(start, size):
        return lax.dot_general(k_ref[start:start + size, :], qs, (((1,), (1,)), ((), ())),
                               preferred_element_type=F32)

    m = acc = None
    st_next = scores(*chunks[0])
    for n, (start, size) in enumerate(chunks):
        st, st_next = st_next, (scores(*chunks[n + 1]) if n + 1 < len(chunks) else None)
        m_chunk = jnp.max(st, axis=0, keepdims=True)
        m_new = m_chunk if m is None else jnp.maximum(m, m_chunk)
        p = jnp.exp2(st - m_new).astype(BF16)
        vt_ones = jnp.concatenate([vt_ref[:, start:start + size], jnp.ones((ONES_ROWS, size), BF16)], axis=0)
        pv = jnp.dot(vt_ones, p, preferred_element_type=F32)
        acc = pv if m is None else jnp.exp2(m - m_new) * acc + pv
        m = m_new
    o = (acc[:HEAD_DIM] / acc[HEAD_DIM:HEAD_DIM + 1]).T
    for g in range(GQA_GROUP):
        sl = slice(g * HEAD_DIM, (g + 1) * HEAD_DIM)
        o_ref[:, sl] = (o[g * tq:(g + 1) * tq] * gate_ref[:, sl].astype(F32)).astype(BF16)


def _attention(q, k, vt, gate, batch, tq, tk=256, cast=None):
    sq = q.shape[0] // batch
    skv = k.shape[0] // batch
    nq = sq // tq
    qspec = pl.BlockSpec((tq, GROUP_W), lambda b, h, i: (b * nq + i, h))
    kspec = pl.BlockSpec((skv, HEAD_DIM), lambda b, h, i: (b, h))
    vspec = pl.BlockSpec((HEAD_DIM, skv), lambda b, h, i: (b * N_KV_HEADS + h, 0))
    ins, in_specs = [q, k, vt, gate], [qspec, kspec, vspec, qspec]
    out_shape, out_specs = [jax.ShapeDtypeStruct(q.shape, BF16)], [qspec]
    if cast is not None:
        w, layer = cast
        _, kdim, ndim = w.shape
        slab = ndim // (batch * N_KV_HEADS * nq)
        step = lambda b, h, i: (b * N_KV_HEADS + h) * nq + i
        ins.append(w)
        in_specs.append(pl.BlockSpec((None, kdim, slab), lambda b, h, i: (layer, 0, step(b, h, i))))
        out_shape.append(jax.ShapeDtypeStruct((1, kdim, ndim), BF16))
        out_specs.append(pl.BlockSpec((None, kdim, slab), lambda b, h, i: (0, 0, step(b, h, i))))
    return pl.pallas_call(
        functools.partial(_attn_body, tq=tq, skv=skv, tk=tk),
        grid=(batch, N_KV_HEADS, nq),
        in_specs=in_specs,
        out_specs=out_specs,
        out_shape=out_shape,
        compiler_params=_params(("arbitrary", "arbitrary", "arbitrary")),
        name="attention",
    )(*ins)


def _mix_body(z_ref, zp_ref, zn_ref, cg_ref, ug_ref, gv_ref, cw_ref, lg_ref, lb_ref, ws_ref, bias_ref,
              yc_ref, yg_ref, *, tr, tiles_per_seq):
    i = pl.program_id(0)
    z = z_ref[...].astype(F32)
    rows = lax.broadcasted_iota(jnp.int32, z.shape, 0)
    prev_row = jnp.where(i % tiles_per_seq == 0, 0.0, zp_ref[7:8, :].astype(F32))
    next_row = jnp.where((i + 1) % tiles_per_seq == 0, 0.0, zn_ref[0:1, :].astype(F32))
    z_prev = jnp.where(rows == 0, prev_row, pltpu.roll(z, 1, 0))
    z_next = jnp.where(rows == tr - 1, next_row, pltpu.roll(z, tr - 1, 0))
    conv = cw_ref[0:1, :] * z_prev + cw_ref[1:2, :] * z + cw_ref[2:3, :] * z_next
    yc_ref[...] = (cg_ref[...].astype(F32) * conv).astype(BF16)

    for c in range(tr // GM_CHUNK):
        rs = slice(c * GM_CHUNK, (c + 1) * GM_CHUNK)
        gv = gv_ref[rs, :].astype(F32)
        mu = jnp.mean(gv, axis=-1, keepdims=True)
        d = gv - mu
        var = jnp.mean(d * d, axis=-1, keepdims=True)
        vn = (d * lax.rsqrt(var + EPS) * lg_ref[...] + lb_ref[...]).astype(BF16)
        for g in range(GM_GROUPS):
            cs = slice(g * GM_GROUP_W, (g + 1) * GM_GROUP_W)
            s = jnp.dot(ws_ref[g], vn[:, cs], preferred_element_type=F32) + bias_ref[g]
            yg_ref[rs, cs] = (ug_ref[rs, cs].astype(F32) * s).astype(BF16)


def _mix(z, cgate, ug, gv, conv_w, ln_g, ln_b, ws, bias, seq, tr):
    m = z.shape[0]
    row = pl.BlockSpec((tr, CONV_W), lambda i: (i, 0))
    r8 = tr // 8
    prev8 = pl.BlockSpec((8, CONV_W), lambda i: (jnp.maximum(i * r8 - 1, 0), 0))
    next8 = pl.BlockSpec((8, CONV_W), lambda i: (jnp.minimum((i + 1) * r8, m // 8 - 1), 0))
    vec = pl.BlockSpec((1, GM_W), lambda i: (0, 0))
    full3 = pl.BlockSpec((GM_GROUPS, GM_CHUNK, GM_CHUNK), lambda i: (0, 0, 0))
    return pl.pallas_call(
        functools.partial(_mix_body, tr=tr, tiles_per_seq=seq // tr),
        grid=(m // tr,),
        in_specs=[row, prev8, next8, row, row, row,
                  pl.BlockSpec((3, CONV_W), lambda i: (0, 0)), vec, vec, full3, full3],
        out_specs=[row, row],
        out_shape=[jax.ShapeDtypeStruct((m, CONV_W), BF16), jax.ShapeDtypeStruct((m, GM_W), BF16)],
        compiler_params=_params(("arbitrary",)),
        name="conv_gmlp",
    )(z, z, z, cgate, ug, gv, conv_w, ln_g.reshape(1, GM_W), ln_b.reshape(1, GM_W), ws, bias)


def _merge_body(ya_ref, yc_ref, yg_ref, wa_ref, wc_ref, wg_ref, ga_ref, gc_ref, gg_ref, o_ref):
    weights = [w_ref[...].astype(BF16) for w_ref in (wa_ref, wc_ref, wg_ref)]
    for rows in _row_blocks(o_ref.shape[0]):
        acc = None
        for gate_ref, y_ref, w in zip((ga_ref, gc_ref, gg_ref), (ya_ref, yc_ref, yg_ref), weights):
            term = gate_ref[rows, :].astype(F32) * jnp.dot(y_ref[rows, :], w, preferred_element_type=F32)
            acc = term if acc is None else acc + term
        o_ref[rows, :] = acc.astype(BF16)


def _merge(ya, yc, yg, w_a, w_c, w_g, gates, layer, tm, tn=512):
    m = ya.shape[0]
    nj = D_MODEL // tn

    def lhs(width):
        return pl.BlockSpec((tm, width), lambda i, j: (i, 0))

    def wspec(width):
        return pl.BlockSpec((None, width, tn), lambda i, j: (layer, 0, j))

    def gspec(branch):
        return pl.BlockSpec((tm, tn), lambda i, j: (i, branch * nj + j))

    return pl.pallas_call(
        _merge_body,
        grid=(m // tm, nj),
        in_specs=[lhs(ATTN_W), lhs(CONV_W), lhs(GM_W), wspec(ATTN_W), wspec(CONV_W), wspec(GM_W),
                  gspec(0), gspec(1), gspec(2)],
        out_specs=pl.BlockSpec((tm, tn), lambda i, j: (i, j)),
        out_shape=jax.ShapeDtypeStruct((m, D_MODEL), BF16),
        compiler_params=_params(("arbitrary", "arbitrary")),
        name="merge",
    )(ya, yc, yg, w_a, w_c, w_g, gates, gates, gates)


def _outproj_body(m_ref, w_ref, x_ref, gt_ref, o_ref):
    w = w_ref[...].astype(BF16)
    for rows in _row_blocks(o_ref.shape[0]):
        acc = jnp.dot(m_ref[rows, :], w, preferred_element_type=F32)
        o_ref[rows, :] = ALPHA * x_ref[rows, :] + gt_ref[...] * acc


def _outproj(merged, w_out, x, gt, layer, row_fn, tm, tn=512):
    m = merged.shape[0]
    return pl.pallas_call(
        _outproj_body,
        grid=(m // tm, D_MODEL // tn),
        in_specs=[
            pl.BlockSpec((tm, D_MODEL), lambda i, j: (i, 0)),
            pl.BlockSpec((None, D_MODEL, tn), lambda i, j: (layer, 0, j)),
            pl.BlockSpec((tm, tn), lambda i, j: (i, j)),
            pl.BlockSpec((None, 1, tn), lambda i, j: (row_fn(i * tm), 0, j)),
        ],
        out_specs=pl.BlockSpec((tm, tn), lambda i, j: (i, j)),
        out_shape=jax.ShapeDtypeStruct((m, D_MODEL), F32),
        compiler_params=_params(("arbitrary", "arbitrary")),
        name="outproj",
    )(merged, w_out, x, gt)


Q_SCALE = HEAD_DIM ** -0.5 * 1.4426950408889634


def _qkv(u, w_in, layer, q_norm, k_norm, rope, seq, tm, want_q=True):
    w_arr, w_idx = w_in
    m = u.shape[0]
    tiles_per_seq = seq // tm
    half = 2 * HEAD_DIM
    gain = pl.BlockSpec((1, HEAD_DIM), lambda i, j: (0, 0))
    mix = pl.BlockSpec((2 * HEAD_DIM, 2 * HEAD_DIM), lambda i, j: (0, 0))
    tab = pl.BlockSpec((tm, HEAD_DIM), lambda i, j: (i % tiles_per_seq, 0))

    def extras(g, scale):
        ex = [(g.reshape(1, HEAD_DIM), gain), (_swap_sumsq_matrix(), mix)]
        if rope is not None:
            ex += [(rope[0] * scale, tab), (rope[1] * scale, tab)]
        return ex

    q = None
    if want_q:
        epi_q = functools.partial(_epi_norm_rope, rope=rope is not None, scale=Q_SCALE)
        q, = _proj(u, w_arr, w_idx, [COL_Q + s * half for s in range(WIDE_TN // half)], ATTN_W, tm, half,
                   epi_q, 1, extras(q_norm[layer], Q_SCALE), tn_out=WIDE_TN, name="proj_q")
    epi_k = functools.partial(_epi_norm_rope, rope=rope is not None, scale=1.0)
    k, = _proj(u, w_arr, w_idx, [COL_K, COL_K + half], KV_W, tm, half, epi_k, 1,
               extras(k_norm[layer], 1.0), tn_out=2 * half, name="proj_k")
    vt_out = (jax.ShapeDtypeStruct((m // seq * KV_W, seq), BF16),
              pl.BlockSpec((KV_W, tm), lambda i, j: (i // tiles_per_seq, i % tiles_per_seq)))
    vt, = _proj(u, w_arr, w_idx, [COL_V], KV_W, tm, KV_W, _epi_transpose, 1, outs=[vt_out], name="proj_v")
    return q, k, vt


def _mixer(u, q, k_all, v_all, wts, w_in, w_mgate, layer, batch, seq, tm, cast=None, cast_small=False):
    w_arr, w_idx = w_in
    w_br, br_idx = [wts["w_br_attn"], wts["w_br_conv"], wts["w_br_gm"]], layer
    w_o = (wts["w_out"], layer)
    agate, = _proj(u, w_arr, w_idx, [COL_AGATE], ATTN_W, tm, WIDE_TN, _epi_silu, 1, name="proj_agate")
    z, cgate, *w_o_b = _proj(u, w_arr, w_idx, [COL_CB, COL_CC, COL_CH, COL_CGATE], CONV_W, tm, 256, _epi_conv,
                             2, side_cast=[w_o] if cast_small else (), name="proj_conv")
    ug, gv, *w_br_b = _proj(u, w_arr, w_idx, [COL_GU, COL_GV, COL_GGATE], GM_W, tm, 256, _epi_gm, 2,
                            side_cast=[(w, layer) for w in w_br] if cast_small else (), name="proj_gm")
    if cast_small:
        w_br, br_idx, w_o = w_br_b, 0, (w_o_b[0], 0)
    w_copy = None
    if len(w_mgate) == 2:
        gates, w_copy = _proj(u, w_mgate[0], w_mgate[1], [COL_MGATE], N_BRANCH * D_MODEL, tm, WIDE_TN // 2,
                              _epi_sigmoid, 1, emit_w=True, name="proj_mgate")
    else:
        gates, = _proj(u, w_mgate[0], w_mgate[1], [w_mgate[2]], N_BRANCH * D_MODEL, tm, WIDE_TN,
                       _epi_sigmoid, 1, name="proj_mgate")
    ya, *w_cast = _attention(q, k_all, v_all, agate, batch, min(seq, 512), cast=cast)
    yc, yg = _mix(z, cgate, ug, gv, wts["conv_w"][layer], wts["gm_ln_g"][layer], wts["gm_ln_b"][layer],
                  wts["gm_ws"][layer], wts["gm_bias"][layer], seq, min(seq, 512))
    merged = _merge(ya, yc, yg, *w_br, gates, br_idx, tm)
    return merged, (w_cast[0] if w_cast else None), w_copy, w_o


def kernel(x, c, ctx, c_ctx, w_ada, b_ada, w_in, q_norm, k_norm, conv_w, gm_ln_g, gm_ln_b, gm_ws, gm_b,
           w_br_attn, w_br_conv, w_br_gm, w_out, ln_g, ln_b):
    batch, seq, _ = x.shape
    ctx_len = ctx.shape[1]
    tm, tm_c = 1024, ctx_len

    w_in_l = (w_in[:1, :, :COL_MGATE].astype(BF16), 0)
    w_mgate = (w_in, 0)
    wts = {
        "w_br_attn": w_br_attn, "w_br_conv": w_br_conv,
        "w_br_gm": w_br_gm, "w_out": w_out, "conv_w": conv_w, "gm_ln_g": gm_ln_g, "gm_ln_b": gm_ln_b,
        "gm_ws": gm_ws.astype(BF16),
        "gm_bias": jnp.broadcast_to(gm_b[:, :, :, None], gm_b.shape + (GM_GROUP_W,)),
    }

    cond = jnp.zeros((MOD_ROWS, D_MODEL), F32).at[:batch].set(c).at[batch].set(c_ctx)
    mod = _modulation(cond, w_ada, b_ada)
    mod = mod.reshape(DEPTH, MOD_ROWS, 1, 3, D_MODEL)
    sh, sc, gt = mod[:, :, :, 0], mod[:, :, :, 1], mod[:, :, :, 2]

    lat_row = lambda r: r // seq
    ctx_row = lambda r: batch

    rope = _rope_tables(seq)
    xr = x.reshape(batch * seq, D_MODEL)
    cr = ctx.reshape(batch * ctx_len, D_MODEL)

    u, = _ln_mod(xr, lat_row, mod=(sc[0], sh[0]), emit_x=False)
    u_c, = _ln_mod(cr, ctx_row, mod=(sc[0], sh[0]), emit_x=False)

    for layer in range(DEPTH):
        last = layer == DEPTH - 1
        q_c, k_c, vt_c = _qkv(u_c, w_in_l, layer, q_norm, k_norm, None, ctx_len, tm_c, want_q=not last)
        if not last:
            merged_c, _, w_copy, w_o = _mixer(u_c, q_c, k_c, vt_c, wts, w_in_l, w_mgate, layer, batch, ctx_len,
                                              batch * ctx_len)
            if w_copy is not None:
                w_mgate = (w_copy, 0, 0)
            pre_c = _outproj(merged_c, w_o[0], cr, gt[layer], w_o[1], ctx_row, batch * ctx_len)
            cr, u_c = _ln_mod(pre_c, ctx_row, ln=(ln_g[layer], ln_b[layer]),
                              mod=(sc[layer + 1], sh[layer + 1]))
        q, k, vt = _qkv(u, w_in_l, layer, q_norm, k_norm, rope, seq, tm)
        k_all = jnp.concatenate([k.reshape(batch, seq, KV_W), k_c.reshape(batch, ctx_len, KV_W)], axis=1)
        k_all = k_all.reshape(batch * (seq + ctx_len), KV_W)
        vt_all = jnp.concatenate([vt, vt_c], axis=1)
        merged, w_next, _, w_o = _mixer(u, q, k_all, vt_all, wts, w_in_l, w_mgate, layer, batch, seq, tm,
                                        cast=None if last else (w_in, layer + 1), cast_small=True)
        w_in_l, w_mgate = (w_next, 0), (w_next, 0, COL_MGATE)
        pre = _outproj(merged, w_o[0], xr, gt[layer], w_o[1], lat_row, tm)
        if last:
            xr, = _ln_mod(pre, lat_row, ln=(ln_g[layer], ln_b[layer]))
        else:
            xr, u = _ln_mod(pre, lat_row, ln=(ln_g[layer], ln_b[layer]), mod=(sc[layer + 1], sh[layer + 1]))
    return xr.reshape(batch, seq, D_MODEL)
```

```python
import functools

import jax
import jax.numpy as jnp
from jax import lax
from jax.experimental import pallas as pl
from jax.experimental.pallas import tpu as pltpu

D_MODEL = 4096
DEPTH = 2
GRID_W = 64
HEAD_DIM = 128
N_Q_HEADS = 16
N_KV_HEADS = 4
GQA_GROUP = N_Q_HEADS // N_KV_HEADS
ATTN_W = N_Q_HEADS * HEAD_DIM
KV_W = N_KV_HEADS * HEAD_DIM
GROUP_W = GQA_GROUP * HEAD_DIM
AXIS_DIM = HEAD_DIM // 2
ROPE_THETA = 10000.0
CONV_W = 1024
GM_W = 1024
GM_GROUPS = 8
GM_GROUP_W = GM_W // GM_GROUPS
GM_CHUNK = 128
N_BRANCH = 3
EPS = 1e-6
ALPHA = (2.0 * DEPTH) ** 0.25

COL_Q = 0
COL_K = ATTN_W
COL_V = COL_K + KV_W
COL_AGATE = COL_V + KV_W
COL_CB = COL_AGATE + ATTN_W
COL_CC = COL_CB + CONV_W
COL_CH = COL_CC + CONV_W
COL_CGATE = COL_CH + CONV_W
COL_GU = COL_CGATE + CONV_W
COL_GV = COL_GU + GM_W
COL_GGATE = COL_GV + GM_W
COL_MGATE = COL_GGATE + GM_W
IN_COLS = COL_MGATE + N_BRANCH * D_MODEL

MOD_ROWS = 8
ONES_ROWS = 16
SUB_M = 512
WIDE_TN = 1024
V7X_VMEM_LIMIT = 56 * 1024 * 1024

BF16 = jnp.bfloat16
F32 = jnp.float32


def _params(sem, flags=None):
    return pltpu.CompilerParams(dimension_semantics=sem, vmem_limit_bytes=V7X_VMEM_LIMIT, flags=flags)


def _silu(x):
    return x * jax.nn.sigmoid(x)


def _modulation_body(c_ref, w_ref, b_ref, o_ref):
    s = _silu(c_ref[...]).astype(BF16)
    o_ref[...] = jnp.dot(s, w_ref[...].astype(BF16), preferred_element_type=F32) + b_ref[...]


def _modulation(cond, w_ada, b_ada, tn=512):
    n = 3 * D_MODEL
    return pl.pallas_call(
        _modulation_body,
        grid=(DEPTH, n // tn),
        in_specs=[
            pl.BlockSpec((MOD_ROWS, D_MODEL), lambda l, j: (0, 0)),
            pl.BlockSpec((None, D_MODEL, tn), lambda l, j: (l, 0, j)),
            pl.BlockSpec((None, 1, tn), lambda l, j: (l, 0, j)),
        ],
        out_specs=pl.BlockSpec((None, MOD_ROWS, tn), lambda l, j: (l, 0, j)),
        out_shape=jax.ShapeDtypeStruct((DEPTH, MOD_ROWS, n), F32),
        compiler_params=_params(("arbitrary", "arbitrary")),
        name="modulation",
    )(cond, w_ada, b_ada.reshape(DEPTH, 1, n))


def _ln_mod_body(*refs, do_ln, emit_x, emit_u):
    it = iter(refs)
    x_ref = next(it)
    g_ref = b_ref = sc_ref = sh_ref = None
    if do_ln:
        g_ref, b_ref = next(it), next(it)
    if emit_u:
        sc_ref, sh_ref = next(it), next(it)
    y = x_ref[...]
    if do_ln:
        mu = jnp.mean(y, axis=-1, keepdims=True)
        d = y - mu
        var = jnp.mean(d * d, axis=-1, keepdims=True)
        y = d * lax.rsqrt(var + EPS) * g_ref[...] + b_ref[...]
    if emit_x:
        next(it)[...] = y
    if emit_u:
        next(it)[...] = (y * (1.0 + sc_ref[...]) + sh_ref[...]).astype(BF16)


def _ln_mod(x, row_fn, ln=None, mod=None, emit_x=True, tr=256):
    m = x.shape[0]
    do_ln, emit_u = ln is not None, mod is not None
    row = pl.BlockSpec((tr, D_MODEL), lambda i: (i, 0))
    vec = pl.BlockSpec((1, D_MODEL), lambda i: (0, 0))
    mvec = pl.BlockSpec((None, 1, D_MODEL), lambda i: (row_fn(i * tr), 0, 0))
    ins, specs = [x], [row]
    if do_ln:
        ins += [ln[0].reshape(1, D_MODEL), ln[1].reshape(1, D_MODEL)]
        specs += [vec, vec]
    if emit_u:
        ins += [mod[0], mod[1]]
        specs += [mvec, mvec]
    out_shape, out_specs = [], []
    if emit_x:
        out_shape.append(jax.ShapeDtypeStruct((m, D_MODEL), F32))
        out_specs.append(row)
    if emit_u:
        out_shape.append(jax.ShapeDtypeStruct((m, D_MODEL), BF16))
        out_specs.append(row)
    return pl.pallas_call(
        functools.partial(_ln_mod_body, do_ln=do_ln, emit_x=emit_x, emit_u=emit_u),
        grid=(m // tr,),
        in_specs=specs,
        out_specs=out_specs,
        out_shape=out_shape,
        compiler_params=_params(("arbitrary",)),
        name="ln_mod",
    )(*ins)


def _proj_body(u_ref, *refs, n_w, n_extra, n_side, epilogue, emit_w):
    w_refs = refs[:n_w]
    extra_refs = refs[n_w:n_w + n_extra]
    side_refs = refs[n_w + n_extra:n_w + n_extra + n_side]
    out_refs = refs[n_w + n_extra + n_side:]
    if n_side:
        out_refs, cast_refs = out_refs[:-n_side], out_refs[-n_side:]
        for src, dst in zip(side_refs, cast_refs):
            dst[...] = src[...].astype(BF16)
    weights = [w[...].astype(BF16) for w in w_refs]
    if emit_w:
        out_refs, wb_ref = out_refs[:-1], out_refs[-1]
        wb_ref[...] = weights[0]
    for rows in _row_blocks(u_ref.shape[0]):
        u = u_ref[rows, :]
        accs = [jnp.dot(u, w, preferred_element_type=F32) for w in weights]
        epilogue(accs, extra_refs, out_refs, rows)


def _proj(u, w, layer, cols, width, tm, tn, epilogue, n_out, extras=(), outs=None, tn_out=None,
          emit_w=False, side_cast=(), name="proj"):
    m, k = u.shape
    tn_out = tn if tn_out is None else tn_out
    stride = tn_out // tn
    ni, nj = m // tm, width // tn_out
    in_specs = [pl.BlockSpec((tm, k), lambda i, j: (i, 0))]
    for c0 in cols:
        in_specs.append(pl.BlockSpec((None, k, tn), lambda i, j, cb=c0 // tn: (layer, 0, cb + j * stride)))
    in_specs += [spec for _, spec in extras]
    if outs is None:
        outs = [(jax.ShapeDtypeStruct((m, width), BF16),
                 pl.BlockSpec((tm, tn_out), lambda i, j: (i, j)))] * n_out
    outs = list(outs)
    if emit_w:
        assert m == tm and len(cols) == 1 and stride == 1
        outs.append((jax.ShapeDtypeStruct((1, k, width), BF16),
                     pl.BlockSpec((None, k, tn), lambda i, j: (0, 0, j))))
    for arr, idx in side_cast:
        _, r, c = arr.shape
        slab = c // (ni * nj)
        in_specs.append(pl.BlockSpec((None, r, slab), lambda i, j, idx=idx: (idx, 0, i * nj + j)))
        outs.append((jax.ShapeDtypeStruct((1, r, c), BF16),
                     pl.BlockSpec((None, r, slab), lambda i, j: (0, 0, i * nj + j))))
    return pl.pallas_call(
        functools.partial(_proj_body, n_w=len(cols), n_extra=len(extras), n_side=len(side_cast),
                          epilogue=epilogue, emit_w=emit_w),
        grid=(ni, nj),
        in_specs=in_specs,
        out_specs=[spec for _, spec in outs],
        out_shape=[shape for shape, _ in outs],
        compiler_params=_params(("arbitrary", "arbitrary")),
        name=name,
    )(u, *([w] * len(cols)), *[a for a, _ in extras], *[a for a, _ in side_cast])


def _row_blocks(tm):
    step = min(tm, SUB_M)
    return [slice(r, r + step) for r in range(0, tm, step)]


def _epi_transpose(accs, extras, outs, rows):
    outs[0][:, rows] = accs[0].T.astype(BF16)


def _epi_silu(accs, extras, outs, rows):
    outs[0][rows, :] = _silu(accs[0]).astype(BF16)


def _epi_sigmoid(accs, extras, outs, rows):
    outs[0][rows, :] = jax.nn.sigmoid(accs[0]).astype(BF16)


def _epi_conv(accs, extras, outs, rows):
    c_b, c_c, c_h, c_gate = accs
    outs[0][rows, :] = (c_c * c_h).astype(BF16)
    outs[1][rows, :] = (c_b * _silu(c_gate)).astype(BF16)


def _epi_gm(accs, extras, outs, rows):
    g_u, g_v, g_gate = accs
    outs[0][rows, :] = (g_u * _silu(g_gate)).astype(BF16)
    outs[1][rows, :] = g_v.astype(BF16)


def _epi_norm_rope(accs, extras, outs, rows, rope, scale):
    gain = extras[0][...]
    mix = extras[1][...]
    if rope:
        cos, sin = extras[2][rows, :], extras[3][rows, :]
    col = 0
    for acc in accs:
        for h in range(acc.shape[1] // HEAD_DIM):
            xh = acc[:, h * HEAD_DIM:(h + 1) * HEAD_DIM]
            y = xh * gain
            lhs = jnp.concatenate([y, xh * xh], axis=1).astype(BF16)
            r = jnp.dot(lhs, mix, preferred_element_type=F32)
            rstd = lax.rsqrt(r[:, HEAD_DIM:] * (1.0 / HEAD_DIM) + EPS)
            if rope:
                out = (y * cos + r[:, :HEAD_DIM] * sin) * rstd
            else:
                out = y * (rstd * scale)
            outs[0][rows, col:col + HEAD_DIM] = out.astype(BF16)
            col += HEAD_DIM


def _swap_sumsq_matrix():
    j = jnp.arange(HEAD_DIM)
    half = AXIS_DIM // 2
    partner = jnp.where(j % AXIS_DIM < half, j + half, j - half)
    perm = (j[:, None] == partner[None, :]).astype(F32)
    zero = jnp.zeros((HEAD_DIM, HEAD_DIM), F32)
    top = jnp.concatenate([perm, zero], axis=1)
    bottom = jnp.concatenate([zero, jnp.ones((HEAD_DIM, HEAD_DIM), F32)], axis=1)
    return jnp.concatenate([top, bottom], axis=0).astype(BF16)


def _rope_tables(n):
    t = jnp.arange(n)
    pos = jnp.stack([t // GRID_W, t % GRID_W], axis=-1).astype(F32)
    freqs = ROPE_THETA ** (-jnp.arange(0, AXIS_DIM, 2, dtype=F32) / AXIS_DIM)
    ang = pos[:, :, None] * freqs
    cos, sin = jnp.cos(ang), jnp.sin(ang)
    cos = jnp.concatenate([cos, cos], axis=-1).reshape(n, HEAD_DIM)
    sin = jnp.concatenate([-sin, sin], axis=-1).reshape(n, HEAD_DIM)
    return cos, sin


def _attn_body(bound_ref, q_ref, k_ref, vt_ref, gate_ref, *rest, tq, skv, tk):
    if len(rest) == 3:
        w_ref, o_ref, wb_ref = rest
        wb_ref[...] = w_ref[...].astype(BF16)
    else:
        o_ref, = rest
    chunks = [(start, min(tk, skv - start)) for start in range(0, skv, tk)]

    def run(shifted):
        qs = jnp.concatenate([q_ref[:, g * HEAD_DIM:(g + 1) * HEAD_DIM] for g in range(GQA_GROUP)], axis=0)

        def scores(start, size):
            return lax.dot_general(k_ref[start:start + size, :], qs, (((1,), (1,)), ((), ())),
                                   preferred_element_type=F32)

        m = acc = None
        st_next = scores(*chunks[0])
        for n, (start, size) in enumerate(chunks):
            st, st_next = st_next, (scores(*chunks[n + 1]) if n + 1 < len(chunks) else None)
            if shifted:
                m_chunk = jnp.max(st, axis=0, keepdims=True)
                m_new = m_chunk if m is None else jnp.maximum(m, m_chunk)
                st = st - m_new
            p = jnp.exp2(st).astype(BF16)
            vt_ones = jnp.concatenate([vt_ref[:, start:start + size], jnp.ones((ONES_ROWS, size), BF16)],
                                      axis=0)
            pv = jnp.dot(vt_ones, p, preferred_element_type=F32)
            if acc is None:
                acc = pv
            elif shifted:
                acc = jnp.exp2(m - m_new) * acc + pv
            else:
                acc = acc + pv
            if shifted:
                m = m_new
        o = (acc[:HEAD_DIM] / acc[HEAD_DIM:HEAD_DIM + 1]).T
        for g in range(GQA_GROUP):
            sl = slice(g * HEAD_DIM, (g + 1) * HEAD_DIM)
            o_ref[:, sl] = (o[g * tq:(g + 1) * tq] * gate_ref[:, sl].astype(F32)).astype(BF16)

    in_range = bound_ref[0, 0] < SCORE_BOUND_LIMIT
    pl.when(in_range)(functools.partial(run, False))
    pl.when(jnp.logical_not(in_range))(functools.partial(run, True))


def _attention(q, k, vt, gate, score_bound, batch, tq, tk=256, cast=None):
    sq = q.shape[0] // batch
    skv = k.shape[0] // batch
    nq = sq // tq
    qspec = pl.BlockSpec((tq, GROUP_W), lambda b, h, i: (b * nq + i, h))
    kspec = pl.BlockSpec((skv, HEAD_DIM), lambda b, h, i: (b, h))
    vspec = pl.BlockSpec((HEAD_DIM, skv), lambda b, h, i: (b * N_KV_HEADS + h, 0))
    ins = [score_bound, q, k, vt, gate]
    in_specs = [pl.BlockSpec(memory_space=pltpu.SMEM), qspec, kspec, vspec, qspec]
    out_shape, out_specs = [jax.ShapeDtypeStruct(q.shape, BF16)], [qspec]
    if cast is not None:
        w, layer = cast
        _, kdim, ndim = w.shape
        slab = ndim // (batch * N_KV_HEADS * nq)
        step = lambda b, h, i: (b * N_KV_HEADS + h) * nq + i
        ins.append(w)
        in_specs.append(pl.BlockSpec((None, kdim, slab), lambda b, h, i: (layer, 0, step(b, h, i))))
        out_shape.append(jax.ShapeDtypeStruct((1, kdim, ndim), BF16))
        out_specs.append(pl.BlockSpec((None, kdim, slab), lambda b, h, i: (0, 0, step(b, h, i))))
    return pl.pallas_call(
        functools.partial(_attn_body, tq=tq, skv=skv, tk=tk),
        grid=(batch, N_KV_HEADS, nq),
        in_specs=in_specs,
        out_specs=out_specs,
        out_shape=out_shape,
        compiler_params=_params(("arbitrary", "arbitrary", "arbitrary")),
        name="attention",
    )(*ins)


def _mix_body(z_ref, zp_ref, zn_ref, cg_ref, ug_ref, gv_ref, cw_ref, lg_ref, lb_ref, ws_ref, bias_ref,
              yc_ref, yg_ref, *, tr, tiles_per_seq):
    i = pl.program_id(0)
    z = z_ref[...].astype(F32)
    rows = lax.broadcasted_iota(jnp.int32, z.shape, 0)
    prev_row = jnp.where(i % tiles_per_seq == 0, 0.0, zp_ref[7:8, :].astype(F32))
    next_row = jnp.where((i + 1) % tiles_per_seq == 0, 0.0, zn_ref[0:1, :].astype(F32))
    z_prev = jnp.where(rows == 0, prev_row, pltpu.roll(z, 1, 0))
    z_next = jnp.where(rows == tr - 1, next_row, pltpu.roll(z, tr - 1, 0))
    conv = cw_ref[0:1, :] * z_prev + cw_ref[1:2, :] * z + cw_ref[2:3, :] * z_next
    yc_ref[...] = (cg_ref[...].astype(F32) * conv).astype(BF16)

    for c in range(tr // GM_CHUNK):
        rs = slice(c * GM_CHUNK, (c + 1) * GM_CHUNK)
        gv = gv_ref[rs, :].astype(F32)
        mu = jnp.mean(gv, axis=-1, keepdims=True)
        d = gv - mu
        var = jnp.mean(d * d, axis=-1, keepdims=True)
        vn = (d * lax.rsqrt(var + EPS) * lg_ref[...] + lb_ref[...]).astype(BF16)
        for g in range(GM_GROUPS):
            cs = slice(g * GM_GROUP_W, (g + 1) * GM_GROUP_W)
            s = jnp.dot(ws_ref[g], vn[:, cs], preferred_element_type=F32) + bias_ref[g]
            yg_ref[rs, cs] = (ug_ref[rs, cs].astype(F32) * s).astype(BF16)


def _mix(z, cgate, ug, gv, conv_w, ln_g, ln_b, ws, bias, seq, tr):
    m = z.shape[0]
    row = pl.BlockSpec((tr, CONV_W), lambda i: (i, 0))
    r8 = tr // 8
    prev8 = pl.BlockSpec((8, CONV_W), lambda i: (jnp.maximum(i * r8 - 1, 0), 0))
    next8 = pl.BlockSpec((8, CONV_W), lambda i: (jnp.minimum((i + 1) * r8, m // 8 - 1), 0))
    vec = pl.BlockSpec((1, GM_W), lambda i: (0, 0))
    full3 = pl.BlockSpec((GM_GROUPS, GM_CHUNK, GM_CHUNK), lambda i: (0, 0, 0))
    return pl.pallas_call(
        functools.partial(_mix_body, tr=tr, tiles_per_seq=seq // tr),
        grid=(m // tr,),
        in_specs=[row, prev8, next8, row, row, row,
                  pl.BlockSpec((3, CONV_W), lambda i: (0, 0)), vec, vec, full3, full3],
        out_specs=[row, row],
        out_shape=[jax.ShapeDtypeStruct((m, CONV_W), BF16), jax.ShapeDtypeStruct((m, GM_W), BF16)],
        compiler_params=_params(("arbitrary",)),
        name="conv_gmlp",
    )(z, z, z, cgate, ug, gv, conv_w, ln_g.reshape(1, GM_W), ln_b.reshape(1, GM_W), ws, bias)


def _merge_body(ya_ref, yc_ref, yg_ref, wa_ref, wc_ref, wg_ref, ga_ref, gc_ref, gg_ref, o_ref):
    weights = [w_ref[...].astype(BF16) for w_ref in (wa_ref, wc_ref, wg_ref)]
    for rows in _row_blocks(o_ref.shape[0]):
        acc = None
        for gate_ref, y_ref, w in zip((ga_ref, gc_ref, gg_ref), (ya_ref, yc_ref, yg_ref), weights):
            term = gate_ref[rows, :].astype(F32) * jnp.dot(y_ref[rows, :], w, preferred_element_type=F32)
            acc = term if acc is None else acc + term
        o_ref[rows, :] = acc.astype(BF16)


def _merge(ya, yc, yg, w_a, w_c, w_g, gates, layer, tm, tn=512):
    m = ya.shape[0]
    nj = D_MODEL // tn

    def lhs(width):
        return pl.BlockSpec((tm, width), lambda i, j: (i, 0))

    def wspec(width):
        return pl.BlockSpec((None, width, tn), lambda i, j: (layer, 0, j))

    def gspec(branch):
        return pl.BlockSpec((tm, tn), lambda i, j: (i, branch * nj + j))

    return pl.pallas_call(
        _merge_body,
        grid=(m // tm, nj),
        in_specs=[lhs(ATTN_W), lhs(CONV_W), lhs(GM_W), wspec(ATTN_W), wspec(CONV_W), wspec(GM_W),
                  gspec(0), gspec(1), gspec(2)],
        out_specs=pl.BlockSpec((tm, tn), lambda i, j: (i, j)),
        out_shape=jax.ShapeDtypeStruct((m, D_MODEL), BF16),
        compiler_params=_params(("arbitrary", "arbitrary")),
        name="merge",
    )(ya, yc, yg, w_a, w_c, w_g, gates, gates, gates)


def _outproj_body(m_ref, w_ref, x_ref, gt_ref, o_ref):
    w = w_ref[...].astype(BF16)
    for rows in _row_blocks(o_ref.shape[0]):
        acc = jnp.dot(m_ref[rows, :], w, preferred_element_type=F32)
        o_ref[rows, :] = ALPHA * x_ref[rows, :] + gt_ref[...] * acc


def _outproj(merged, w_out, x, gt, layer, row_fn, tm, tn=512):
    m = merged.shape[0]
    return pl.pallas_call(
        _outproj_body,
        grid=(m // tm, D_MODEL // tn),
        in_specs=[
            pl.BlockSpec((tm, D_MODEL), lambda i, j: (i, 0)),
            pl.BlockSpec((None, D_MODEL, tn), lambda i, j: (layer, 0, j)),
            pl.BlockSpec((tm, tn), lambda i, j: (i, j)),
            pl.BlockSpec((None, 1, tn), lambda i, j: (row_fn(i * tm), 0, j)),
        ],
        out_specs=pl.BlockSpec((tm, tn), lambda i, j: (i, j)),
        out_shape=jax.ShapeDtypeStruct((m, D_MODEL), F32),
        compiler_params=_params(("arbitrary", "arbitrary")),
        name="outproj",
    )(merged, w_out, x, gt)


Q_SCALE = HEAD_DIM ** -0.5 * 1.4426950408889634
SCORE_BOUND_LIMIT = 60.0
SCORE_BOUND_SLACK = 1.05


def _score_bound(q_gain, k_gain):
    bound = SCORE_BOUND_SLACK * HEAD_DIM * Q_SCALE * jnp.max(jnp.abs(q_gain)) * jnp.max(jnp.abs(k_gain))
    return bound.reshape(1, 1).astype(F32)


def _qkv(u, w_in, layer, q_norm, k_norm, rope, seq, tm, want_q=True):
    w_arr, w_idx = w_in
    m = u.shape[0]
    tiles_per_seq = seq // tm
    half = 2 * HEAD_DIM
    gain = pl.BlockSpec((1, HEAD_DIM), lambda i, j: (0, 0))
    mix = pl.BlockSpec((2 * HEAD_DIM, 2 * HEAD_DIM), lambda i, j: (0, 0))
    tab = pl.BlockSpec((tm, HEAD_DIM), lambda i, j: (i % tiles_per_seq, 0))

    def extras(g, scale):
        ex = [(g.reshape(1, HEAD_DIM), gain), (_swap_sumsq_matrix(), mix)]
        if rope is not None:
            ex += [(rope[0] * scale, tab), (rope[1] * scale, tab)]
        return ex

    q = None
    if want_q:
        epi_q = functools.partial(_epi_norm_rope, rope=rope is not None, scale=Q_SCALE)
        q, = _proj(u, w_arr, w_idx, [COL_Q + s * half for s in range(WIDE_TN // half)], ATTN_W, tm, half,
                   epi_q, 1, extras(q_norm[layer], Q_SCALE), tn_out=WIDE_TN, name="proj_q")
    epi_k = functools.partial(_epi_norm_rope, rope=rope is not None, scale=1.0)
    k, = _proj(u, w_arr, w_idx, [COL_K, COL_K + half], KV_W, tm, half, epi_k, 1,
               extras(k_norm[layer], 1.0), tn_out=2 * half, name="proj_k")
    vt_out = (jax.ShapeDtypeStruct((m // seq * KV_W, seq), BF16),
              pl.BlockSpec((KV_W, tm), lambda i, j: (i // tiles_per_seq, i % tiles_per_seq)))
    vt, = _proj(u, w_arr, w_idx, [COL_V], KV_W, tm, KV_W, _epi_transpose, 1, outs=[vt_out], name="proj_v")
    return q, k, vt


def _mixer(u, q, k_all, v_all, wts, w_in, w_mgate, layer, batch, seq, tm, cast=None, cast_small=False):
    w_arr, w_idx = w_in
    w_br, br_idx = [wts["w_br_attn"], wts["w_br_conv"], wts["w_br_gm"]], layer
    w_o = (wts["w_out"], layer)
    agate, = _proj(u, w_arr, w_idx, [COL_AGATE], ATTN_W, tm, WIDE_TN, _epi_silu, 1, name="proj_agate")
    z, cgate, *w_o_b = _proj(u, w_arr, w_idx, [COL_CB, COL_CC, COL_CH, COL_CGATE], CONV_W, tm, 256, _epi_conv,
                             2, side_cast=[w_o] if cast_small else (), name="proj_conv")
    ug, gv, *w_br_b = _proj(u, w_arr, w_idx, [COL_GU, COL_GV, COL_GGATE], GM_W, tm, 256, _epi_gm, 2,
                            side_cast=[(w, layer) for w in w_br] if cast_small else (), name="proj_gm")
    if cast_small:
        w_br, br_idx, w_o = w_br_b, 0, (w_o_b[0], 0)
    w_copy = None
    if len(w_mgate) == 2:
        gates, w_copy = _proj(u, w_mgate[0], w_mgate[1], [COL_MGATE], N_BRANCH * D_MODEL, tm, WIDE_TN // 2,
                              _epi_sigmoid, 1, emit_w=True, name="proj_mgate")
    else:
        gates, = _proj(u, w_mgate[0], w_mgate[1], [w_mgate[2]], N_BRANCH * D_MODEL, tm, WIDE_TN,
                       _epi_sigmoid, 1, name="proj_mgate")
    ya, *w_cast = _attention(q, k_all, v_all, agate, wts["score_bound"][layer], batch, min(seq, 512), cast=cast)
    yc, yg = _mix(z, cgate, ug, gv, wts["conv_w"][layer], wts["gm_ln_g"][layer], wts["gm_ln_b"][layer],
                  wts["gm_ws"][layer], wts["gm_bias"][layer], seq, min(seq, 512))
    merged = _merge(ya, yc, yg, *w_br, gates, br_idx, tm)
    return merged, (w_cast[0] if w_cast else None), w_copy, w_o


def kernel(x, c, ctx, c_ctx, w_ada, b_ada, w_in, q_norm, k_norm, conv_w, gm_ln_g, gm_ln_b, gm_ws, gm_b,
           w_br_attn, w_br_conv, w_br_gm, w_out, ln_g, ln_b):
    batch, seq, _ = x.shape
    ctx_len = ctx.shape[1]
    tm, tm_c = 1024, ctx_len

    w_in_l = (w_in[:1, :, :COL_MGATE].astype(BF16), 0)
    w_mgate = (w_in, 0)
    wts = {
        "w_br_attn": w_br_attn, "w_br_conv": w_br_conv,
        "w_br_gm": w_br_gm, "w_out": w_out, "conv_w": conv_w, "gm_ln_g": gm_ln_g, "gm_ln_b": gm_ln_b,
        "gm_ws": gm_ws.astype(BF16),
        "gm_bias": jnp.broadcast_to(gm_b[:, :, :, None], gm_b.shape + (GM_GROUP_W,)),
        "score_bound": [_score_bound(q_norm[l], k_norm[l]) for l in range(DEPTH)],
    }

    cond = jnp.zeros((MOD_ROWS, D_MODEL), F32).at[:batch].set(c).at[batch].set(c_ctx)
    mod = _modulation(cond, w_ada, b_ada)
    mod = mod.reshape(DEPTH, MOD_ROWS, 1, 3, D_MODEL)
    sh, sc, gt = mod[:, :, :, 0], mod[:, :, :, 1], mod[:, :, :, 2]

    lat_row = lambda r: r // seq
    ctx_row = lambda r: batch

    rope = _rope_tables(seq)
    xr = x.reshape(batch * seq, D_MODEL)
    cr = ctx.reshape(batch * ctx_len, D_MODEL)

    u, = _ln_mod(xr, lat_row, mod=(sc[0], sh[0]), emit_x=False)
    u_c, = _ln_mod(cr, ctx_row, mod=(sc[0], sh[0]), emit_x=False)

    for layer in range(DEPTH):
        last = layer == DEPTH - 1
        q_c, k_c, vt_c = _qkv(u_c, w_in_l, layer, q_norm, k_norm, None, ctx_len, tm_c, want_q=not last)
        if not last:
            merged_c, _, w_copy, w_o = _mixer(u_c, q_c, k_c, vt_c, wts, w_in_l, w_mgate, layer, batch, ctx_len,
                                              batch * ctx_len)
            if w_copy is not None:
                w_mgate = (w_copy, 0, 0)
            pre_c = _outproj(merged_c, w_o[0], cr, gt[layer], w_o[1], ctx_row, batch * ctx_len)
            cr, u_c = _ln_mod(pre_c, ctx_row, ln=(ln_g[layer], ln_b[layer]),
                              mod=(sc[layer + 1], sh[layer + 1]))
        q, k, vt = _qkv(u, w_in_l, layer, q_norm, k_norm, rope, seq, tm)
        k_all = jnp.concatenate([k.reshape(batch, seq, KV_W), k_c.reshape(batch, ctx_len, KV_W)], axis=1)
        k_all = k_all.reshape(batch * (seq + ctx_len), KV_W)
        vt_all = jnp.concatenate([vt, vt_c], axis=1)
        merged, w_next, _, w_o = _mixer(u, q, k_all, vt_all, wts, w_in_l, w_mgate, layer, batch, seq, tm,
                                        cast=None if last else (w_in, layer + 1), cast_small=True)
        w_in_l, w_mgate = (w_next, 0), (w_next, 0, COL_MGATE)
        pre = _outproj(merged, w_o[0], xr, gt[layer], w_o[1], lat_row, tm)
        if last:
            xr, = _ln_mod(pre, lat_row, ln=(ln_g[layer], ln_b[layer]))
        else:
            xr, u = _ln_mod(pre, lat_row, ln=(ln_g[layer], ln_b[layer]), mod=(sc[layer + 1], sh[layer + 1]))
    return xr.reshape(batch, seq, D_MODEL)
```

```python
import functools

import jax
import jax.numpy as jnp
from jax import lax
from jax.experimental import pallas as pl
from jax.experimental.pallas import tpu as pltpu

D_MODEL = 4096
DEPTH = 2
GRID_W = 64
HEAD_DIM = 128
N_Q_HEADS = 16
N_KV_HEADS = 4
GQA_GROUP = N_Q_HEADS // N_KV_HEADS
ATTN_W = N_Q_HEADS * HEAD_DIM
KV_W = N_KV_HEADS * HEAD_DIM
GROUP_W = GQA_GROUP * HEAD_DIM
AXIS_DIM = HEAD_DIM // 2
ROPE_THETA = 10000.0
CONV_W = 1024
GM_W = 1024
GM_GROUPS = 8
GM_GROUP_W = GM_W // GM_GROUPS
GM_CHUNK = 128
N_BRANCH = 3
EPS = 1e-6
ALPHA = (2.0 * DEPTH) ** 0.25

COL_Q = 0
COL_K = ATTN_W
COL_V = COL_K + KV_W
COL_AGATE = COL_V + KV_W
COL_CB = COL_AGATE + ATTN_W
COL_CC = COL_CB + CONV_W
COL_CH = COL_CC + CONV_W
COL_CGATE = COL_CH + CONV_W
COL_GU = COL_CGATE + CONV_W
COL_GV = COL_GU + GM_W
COL_GGATE = COL_GV + GM_W
COL_MGATE = COL_GGATE + GM_W
IN_COLS = COL_MGATE + N_BRANCH * D_MODEL

MOD_ROWS = 8
ONES_ROWS = 16
SUB_M = 512
WIDE_TN = 1024
V7X_VMEM_LIMIT = 56 * 1024 * 1024

BF16 = jnp.bfloat16
F32 = jnp.float32


def _params(sem, flags=None):
    return pltpu.CompilerParams(dimension_semantics=sem, vmem_limit_bytes=V7X_VMEM_LIMIT, flags=flags)


def _sigmoid(x):
    return 0.5 * jnp.tanh(0.5 * x) + 0.5


def _silu(x):
    return x * _sigmoid(x)


def _modulation_body(c_ref, w_ref, b_ref, o_ref):
    s = _silu(c_ref[...]).astype(BF16)
    o_ref[...] = jnp.dot(s, w_ref[...].astype(BF16), preferred_element_type=F32) + b_ref[...]


def _modulation(cond, w_ada, b_ada, tn=512):
    n = 3 * D_MODEL
    return pl.pallas_call(
        _modulation_body,
        grid=(DEPTH, n // tn),
        in_specs=[
            pl.BlockSpec((MOD_ROWS, D_MODEL), lambda l, j: (0, 0)),
            pl.BlockSpec((None, D_MODEL, tn), lambda l, j: (l, 0, j)),
            pl.BlockSpec((None, 1, tn), lambda l, j: (l, 0, j)),
        ],
        out_specs=pl.BlockSpec((None, MOD_ROWS, tn), lambda l, j: (l, 0, j)),
        out_shape=jax.ShapeDtypeStruct((DEPTH, MOD_ROWS, n), F32),
        compiler_params=_params(("arbitrary", "arbitrary")),
        name="modulation",
    )(cond, w_ada, b_ada.reshape(DEPTH, 1, n))


def _ln_mod_body(*refs, do_ln, emit_x, emit_u):
    it = iter(refs)
    x_ref = next(it)
    g_ref = b_ref = sc_ref = sh_ref = None
    if do_ln:
        g_ref, b_ref = next(it), next(it)
    if emit_u:
        sc_ref, sh_ref = next(it), next(it)
    y = x_ref[...]
    if do_ln:
        mu = jnp.mean(y, axis=-1, keepdims=True)
        d = y - mu
        var = jnp.mean(d * d, axis=-1, keepdims=True)
        y = d * lax.rsqrt(var + EPS) * g_ref[...] + b_ref[...]
    if emit_x:
        next(it)[...] = y
    if emit_u:
        next(it)[...] = (y * (1.0 + sc_ref[...]) + sh_ref[...]).astype(BF16)


def _ln_mod(x, row_fn, ln=None, mod=None, emit_x=True, tr=512):
    m = x.shape[0]
    do_ln, emit_u = ln is not None, mod is not None
    row = pl.BlockSpec((tr, D_MODEL), lambda i: (i, 0))
    vec = pl.BlockSpec((1, D_MODEL), lambda i: (0, 0))
    mvec = pl.BlockSpec((None, 1, D_MODEL), lambda i: (row_fn(i * tr), 0, 0))
    ins, specs = [x], [row]
    if do_ln:
        ins += [ln[0].reshape(1, D_MODEL), ln[1].reshape(1, D_MODEL)]
        specs += [vec, vec]
    if emit_u:
        ins += [mod[0], mod[1]]
        specs += [mvec, mvec]
    out_shape, out_specs = [], []
    if emit_x:
        out_shape.append(jax.ShapeDtypeStruct((m, D_MODEL), F32))
        out_specs.append(row)
    if emit_u:
        out_shape.append(jax.ShapeDtypeStruct((m, D_MODEL), BF16))
        out_specs.append(row)
    return pl.pallas_call(
        functools.partial(_ln_mod_body, do_ln=do_ln, emit_x=emit_x, emit_u=emit_u),
        grid=(m // tr,),
        in_specs=specs,
        out_specs=out_specs,
        out_shape=out_shape,
        compiler_params=_params(("arbitrary",)),
        name="ln_mod",
    )(*ins)


def _proj_body(u_ref, *refs, n_w, n_extra, n_side, epilogue, emit_w):
    w_refs = refs[:n_w]
    extra_refs = refs[n_w:n_w + n_extra]
    side_refs = refs[n_w + n_extra:n_w + n_extra + n_side]
    out_refs = refs[n_w + n_extra + n_side:]
    if n_side:
        out_refs, cast_refs = out_refs[:-n_side], out_refs[-n_side:]
        for src, dst in zip(side_refs, cast_refs):
            dst[...] = src[...].astype(BF16)
    weights = [w[...].astype(BF16) for w in w_refs]
    if emit_w:
        out_refs, wb_ref = out_refs[:-1], out_refs[-1]
        wb_ref[...] = weights[0]
    for rows in _row_blocks(u_ref.shape[0]):
        u = u_ref[rows, :]
        accs = [jnp.dot(u, w, preferred_element_type=F32) for w in weights]
        epilogue(accs, extra_refs, out_refs, rows)


def _proj(u, w, layer, cols, width, tm, tn, epilogue, n_out, extras=(), outs=None, tn_out=None,
          emit_w=False, side_cast=(), name="proj"):
    m, k = u.shape
    tn_out = tn if tn_out is None else tn_out
    stride = tn_out // tn
    ni, nj = m // tm, width // tn_out
    in_specs = [pl.BlockSpec((tm, k), lambda i, j: (i, 0))]
    for c0 in cols:
        in_specs.append(pl.BlockSpec((None, k, tn), lambda i, j, cb=c0 // tn: (layer, 0, cb + j * stride)))
    in_specs += [spec for _, spec in extras]
    if outs is None:
        outs = [(jax.ShapeDtypeStruct((m, width), BF16),
                 pl.BlockSpec((tm, tn_out), lambda i, j: (i, j)))] * n_out
    outs = list(outs)
    if emit_w:
        assert m == tm and len(cols) == 1 and stride == 1
        outs.append((jax.ShapeDtypeStruct((1, k, width), BF16),
                     pl.BlockSpec((None, k, tn), lambda i, j: (0, 0, j))))
    for arr, idx in side_cast:
        _, r, c = arr.shape
        slab = c // (ni * nj)
        in_specs.append(pl.BlockSpec((None, r, slab), lambda i, j, idx=idx: (idx, 0, i * nj + j)))
        outs.append((jax.ShapeDtypeStruct((1, r, c), BF16),
                     pl.BlockSpec((None, r, slab), lambda i, j: (0, 0, i * nj + j))))
    return pl.pallas_call(
        functools.partial(_proj_body, n_w=len(cols), n_extra=len(extras), n_side=len(side_cast),
                          epilogue=epilogue, emit_w=emit_w),
        grid=(ni, nj),
        in_specs=in_specs,
        out_specs=[spec for _, spec in outs],
        out_shape=[shape for shape, _ in outs],
        compiler_params=_params(("arbitrary", "arbitrary")),
        name=name,
    )(u, *([w] * len(cols)), *[a for a, _ in extras], *[a for a, _ in side_cast])


def _row_blocks(tm):
    step = min(tm, SUB_M)
    return [slice(r, r + step) for r in range(0, tm, step)]


def _epi_transpose(accs, extras, outs, rows):
    outs[0][:, rows] = accs[0].T.astype(BF16)


def _epi_silu(accs, extras, outs, rows):
    outs[0][rows, :] = _silu(accs[0]).astype(BF16)


def _epi_sigmoid(accs, extras, outs, rows):
    outs[0][rows, :] = _sigmoid(accs[0]).astype(BF16)


def _epi_conv(accs, extras, outs, rows):
    c_b, c_c, c_h, c_gate = accs
    outs[0][rows, :] = (c_c * c_h).astype(BF16)
    outs[1][rows, :] = (c_b * _silu(c_gate)).astype(BF16)


def _epi_gm(accs, extras, outs, rows):
    g_u, g_v, g_gate = accs
    outs[0][rows, :] = (g_u * _silu(g_gate)).astype(BF16)
    outs[1][rows, :] = g_v.astype(BF16)


def _epi_norm_rope(accs, extras, outs, rows, rope, scale):
    gain = extras[0][...]
    mix = extras[1][...]
    if rope:
        cos, sin = extras[2][rows, :], extras[3][rows, :]
    col = 0
    for acc in accs:
        for h in range(acc.shape[1] // HEAD_DIM):
            xh = acc[:, h * HEAD_DIM:(h + 1) * HEAD_DIM]
            y = xh * gain
            lhs = jnp.concatenate([y, xh * xh], axis=1).astype(BF16)
            r = jnp.dot(lhs, mix, preferred_element_type=F32)
            rstd = lax.rsqrt(r[:, HEAD_DIM:] * (1.0 / HEAD_DIM) + EPS)
            if rope:
                out = (y * cos + r[:, :HEAD_DIM] * sin) * rstd
            else:
                out = y * (rstd * scale)
            outs[0][rows, col:col + HEAD_DIM] = out.astype(BF16)
            col += HEAD_DIM


def _swap_sumsq_matrix():
    j = jnp.arange(HEAD_DIM)
    half = AXIS_DIM // 2
    partner = jnp.where(j % AXIS_DIM < half, j + half, j - half)
    perm = (j[:, None] == partner[None, :]).astype(F32)
    zero = jnp.zeros((HEAD_DIM, HEAD_DIM), F32)
    top = jnp.concatenate([perm, zero], axis=1)
    bottom = jnp.concatenate([zero, jnp.ones((HEAD_DIM, HEAD_DIM), F32)], axis=1)
    return jnp.concatenate([top, bottom], axis=0).astype(BF16)


def _rope_tables(n):
    t = jnp.arange(n)
    pos = jnp.stack([t // GRID_W, t % GRID_W], axis=-1).astype(F32)
    freqs = ROPE_THETA ** (-jnp.arange(0, AXIS_DIM, 2, dtype=F32) / AXIS_DIM)
    ang = pos[:, :, None] * freqs
    cos, sin = jnp.cos(ang), jnp.sin(ang)
    cos = jnp.concatenate([cos, cos], axis=-1).reshape(n, HEAD_DIM)
    sin = jnp.concatenate([-sin, sin], axis=-1).reshape(n, HEAD_DIM)
    return cos, sin


def _attn_body(bound_ref, q_ref, k_ref, vt_ref, gate_ref, *rest, tq, skv, tk):
    if len(rest) == 3:
        w_ref, o_ref, wb_ref = rest
        wb_ref[...] = w_ref[...].astype(BF16)
    else:
        o_ref, = rest
    chunks = [(start, min(tk, skv - start)) for start in range(0, skv, tk)]

    def run(shifted):
        qs = jnp.concatenate([q_ref[:, g * HEAD_DIM:(g + 1) * HEAD_DIM] for g in range(GQA_GROUP)], axis=0)

        def scores(start, size):
            return lax.dot_general(k_ref[start:start + size, :], qs, (((1,), (1,)), ((), ())),
                                   preferred_element_type=F32)

        m = acc = None
        st_next = scores(*chunks[0])
        for n, (start, size) in enumerate(chunks):
            st, st_next = st_next, (scores(*chunks[n + 1]) if n + 1 < len(chunks) else None)
            if shifted:
                m_chunk = jnp.max(st, axis=0, keepdims=True)
                m_new = m_chunk if m is None else jnp.maximum(m, m_chunk)
                st = st - m_new
            p = jnp.exp2(st).astype(BF16)
            vt_ones = jnp.concatenate([vt_ref[:, start:start + size], jnp.ones((ONES_ROWS, size), BF16)],
                                      axis=0)
            pv = jnp.dot(vt_ones, p, preferred_element_type=F32)
            if acc is None:
                acc = pv
            elif shifted:
                acc = jnp.exp2(m - m_new) * acc + pv
            else:
                acc = acc + pv
            if shifted:
                m = m_new
        o = (acc[:HEAD_DIM] / acc[HEAD_DIM:HEAD_DIM + 1]).T
        for g in range(GQA_GROUP):
            sl = slice(g * HEAD_DIM, (g + 1) * HEAD_DIM)
            o_ref[:, sl] = (o[g * tq:(g + 1) * tq] * gate_ref[:, sl].astype(F32)).astype(BF16)

    in_range = bound_ref[0, 0] < SCORE_BOUND_LIMIT
    pl.when(in_range)(functools.partial(run, False))
    pl.when(jnp.logical_not(in_range))(functools.partial(run, True))


def _attention(q, k, vt, gate, score_bound, batch, tq, tk=256, cast=None):
    sq = q.shape[0] // batch
    skv = k.shape[0] // batch
    nq = sq // tq
    qspec = pl.BlockSpec((tq, GROUP_W), lambda b, h, i: (b * nq + i, h))
    kspec = pl.BlockSpec((skv, HEAD_DIM), lambda b, h, i: (b, h))
    vspec = pl.BlockSpec((HEAD_DIM, skv), lambda b, h, i: (b * N_KV_HEADS + h, 0))
    ins = [score_bound, q, k, vt, gate]
    in_specs = [pl.BlockSpec(memory_space=pltpu.SMEM), qspec, kspec, vspec, qspec]
    out_shape, out_specs = [jax.ShapeDtypeStruct(q.shape, BF16)], [qspec]
    if cast is not None:
        w, layer = cast
        _, kdim, ndim = w.shape
        slab = ndim // (batch * N_KV_HEADS * nq)
        step = lambda b, h, i: (b * N_KV_HEADS + h) * nq + i
        ins.append(w)
        in_specs.append(pl.BlockSpec((None, kdim, slab), lambda b, h, i: (layer, 0, step(b, h, i))))
        out_shape.append(jax.ShapeDtypeStruct((1, kdim, ndim), BF16))
        out_specs.append(pl.BlockSpec((None, kdim, slab), lambda b, h, i: (0, 0, step(b, h, i))))
    return pl.pallas_call(
        functools.partial(_attn_body, tq=tq, skv=skv, tk=tk),
        grid=(batch, N_KV_HEADS, nq),
        in_specs=in_specs,
        out_specs=out_specs,
        out_shape=out_shape,
        compiler_params=_params(("arbitrary", "arbitrary", "arbitrary")),
        name="attention",
    )(*ins)


def _mix_body(z_ref, zp_ref, zn_ref, cg_ref, ug_ref, gv_ref, cw_ref, lg_ref, lb_ref, ws_ref, bias_ref,
              yc_ref, yg_ref, *, tr, tiles_per_seq):
    i = pl.program_id(0)
    z = z_ref[...].astype(F32)
    rows = lax.broadcasted_iota(jnp.int32, z.shape, 0)
    prev_row = jnp.where(i % tiles_per_seq == 0, 0.0, zp_ref[7:8, :].astype(F32))
    next_row = jnp.where((i + 1) % tiles_per_seq == 0, 0.0, zn_ref[0:1, :].astype(F32))
    z_prev = jnp.where(rows == 0, prev_row, pltpu.roll(z, 1, 0))
    z_next = jnp.where(rows == tr - 1, next_row, pltpu.roll(z, tr - 1, 0))
    conv = cw_ref[0:1, :] * z_prev + cw_ref[1:2, :] * z + cw_ref[2:3, :] * z_next
    yc_ref[...] = (cg_ref[...].astype(F32) * conv).astype(BF16)

    for c in range(tr // GM_CHUNK):
        rs = slice(c * GM_CHUNK, (c + 1) * GM_CHUNK)
        gv = gv_ref[rs, :].astype(F32)
        mu = jnp.mean(gv, axis=-1, keepdims=True)
        d = gv - mu
        var = jnp.mean(d * d, axis=-1, keepdims=True)
        vn = (d * lax.rsqrt(var + EPS) * lg_ref[...] + lb_ref[...]).astype(BF16)
        for g in range(GM_GROUPS):
            cs = slice(g * GM_GROUP_W, (g + 1) * GM_GROUP_W)
            s = jnp.dot(ws_ref[g], vn[:, cs], preferred_element_type=F32) + bias_ref[g]
            yg_ref[rs, cs] = (ug_ref[rs, cs].astype(F32) * s).astype(BF16)


def _mix(z, cgate, ug, gv, conv_w, ln_g, ln_b, ws, bias, seq, tr):
    m = z.shape[0]
    row = pl.BlockSpec((tr, CONV_W), lambda i: (i, 0))
    r8 = tr // 8
    prev8 = pl.BlockSpec((8, CONV_W), lambda i: (jnp.maximum(i * r8 - 1, 0), 0))
    next8 = pl.BlockSpec((8, CONV_W), lambda i: (jnp.minimum((i + 1) * r8, m // 8 - 1), 0))
    vec = pl.BlockSpec((1, GM_W), lambda i: (0, 0))
    full3 = pl.BlockSpec((GM_GROUPS, GM_CHUNK, GM_CHUNK), lambda i: (0, 0, 0))
    return pl.pallas_call(
        functools.partial(_mix_body, tr=tr, tiles_per_seq=seq // tr),
        grid=(m // tr,),
        in_specs=[row, prev8, next8, row, row, row,
                  pl.BlockSpec((3, CONV_W), lambda i: (0, 0)), vec, vec, full3, full3],
        out_specs=[row, row],
        out_shape=[jax.ShapeDtypeStruct((m, CONV_W), BF16), jax.ShapeDtypeStruct((m, GM_W), BF16)],
        compiler_params=_params(("arbitrary",)),
        name="conv_gmlp",
    )(z, z, z, cgate, ug, gv, conv_w, ln_g.reshape(1, GM_W), ln_b.reshape(1, GM_W), ws, bias)


def _merge_body(ya_ref, yc_ref, yg_ref, wa_ref, wc_ref, wg_ref, ga_ref, gc_ref, gg_ref, o_ref):
    weights = [w_ref[...].astype(BF16) for w_ref in (wa_ref, wc_ref, wg_ref)]
    for rows in _row_blocks(o_ref.shape[0]):
        acc = None
        for gate_ref, y_ref, w in zip((ga_ref, gc_ref, gg_ref), (ya_ref, yc_ref, yg_ref), weights):
            term = gate_ref[rows, :].astype(F32) * jnp.dot(y_ref[rows, :], w, preferred_element_type=F32)
            acc = term if acc is None else acc + term
        o_ref[rows, :] = acc.astype(BF16)


def _merge(ya, yc, yg, w_a, w_c, w_g, gates, layer, tm, tn=512):
    m = ya.shape[0]
    nj = D_MODEL // tn

    def lhs(width):
        return pl.BlockSpec((tm, width), lambda i, j: (i, 0))

    def wspec(width):
        return pl.BlockSpec((None, width, tn), lambda i, j: (layer, 0, j))

    def gspec(branch):
        return pl.BlockSpec((tm, tn), lambda i, j: (i, branch * nj + j))

    return pl.pallas_call(
        _merge_body,
        grid=(m // tm, nj),
        in_specs=[lhs(ATTN_W), lhs(CONV_W), lhs(GM_W), wspec(ATTN_W), wspec(CONV_W), wspec(GM_W),
                  gspec(0), gspec(1), gspec(2)],
        out_specs=pl.BlockSpec((tm, tn), lambda i, j: (i, j)),
        out_shape=jax.ShapeDtypeStruct((m, D_MODEL), BF16),
        compiler_params=_params(("arbitrary", "arbitrary")),
        name="merge",
    )(ya, yc, yg, w_a, w_c, w_g, gates, gates, gates)


def _outproj_body(m_ref, w_ref, x_ref, gt_ref, o_ref):
    w = w_ref[...].astype(BF16)
    for rows in _row_blocks(o_ref.shape[0]):
        acc = jnp.dot(m_ref[rows, :], w, preferred_element_type=F32)
        o_ref[rows, :] = ALPHA * x_ref[rows, :] + gt_ref[...] * acc


def _outproj(merged, w_out, x, gt, layer, row_fn, tm, tn=512):
    m = merged.shape[0]
    return pl.pallas_call(
        _outproj_body,
        grid=(m // tm, D_MODEL // tn),
        in_specs=[
            pl.BlockSpec((tm, D_MODEL), lambda i, j: (i, 0)),
            pl.BlockSpec((None, D_MODEL, tn), lambda i, j: (layer, 0, j)),
            pl.BlockSpec((tm, tn), lambda i, j: (i, j)),
            pl.BlockSpec((None, 1, tn), lambda i, j: (row_fn(i * tm), 0, j)),
        ],
        out_specs=pl.BlockSpec((tm, tn), lambda i, j: (i, j)),
        out_shape=jax.ShapeDtypeStruct((m, D_MODEL), F32),
        compiler_params=_params(("arbitrary", "arbitrary")),
        name="outproj",
    )(merged, w_out, x, gt)


Q_SCALE = HEAD_DIM ** -0.5 * 1.4426950408889634
SCORE_BOUND_LIMIT = 60.0
SCORE_BOUND_SLACK = 1.05


def _score_bound(q_gain, k_gain):
    bound = SCORE_BOUND_SLACK * HEAD_DIM * Q_SCALE * jnp.max(jnp.abs(q_gain)) * jnp.max(jnp.abs(k_gain))
    return bound.reshape(1, 1).astype(F32)


def _qkv(u, w_in, layer, q_norm, k_norm, rope, seq, tm, want_q=True):
    w_arr, w_idx = w_in
    m = u.shape[0]
    tiles_per_seq = seq // tm
    half = 2 * HEAD_DIM
    gain = pl.BlockSpec((1, HEAD_DIM), lambda i, j: (0, 0))
    mix = pl.BlockSpec((2 * HEAD_DIM, 2 * HEAD_DIM), lambda i, j: (0, 0))
    tab = pl.BlockSpec((tm, HEAD_DIM), lambda i, j: (i % tiles_per_seq, 0))

    def extras(g, scale):
        ex = [(g.reshape(1, HEAD_DIM), gain), (_swap_sumsq_matrix(), mix)]
        if rope is not None:
            ex += [(rope[0] * scale, tab), (rope[1] * scale, tab)]
        return ex

    q = None
    if want_q:
        epi_q = functools.partial(_epi_norm_rope, rope=rope is not None, scale=Q_SCALE)
        q, = _proj(u, w_arr, w_idx, [COL_Q + s * half for s in range(WIDE_TN // half)], ATTN_W, tm, half,
                   epi_q, 1, extras(q_norm[layer], Q_SCALE), tn_out=WIDE_TN, name="proj_q")
    epi_k = functools.partial(_epi_norm_rope, rope=rope is not None, scale=1.0)
    k, = _proj(u, w_arr, w_idx, [COL_K, COL_K + half], KV_W, tm, half, epi_k, 1,
               extras(k_norm[layer], 1.0), tn_out=2 * half, name="proj_k")
    vt_out = (jax.ShapeDtypeStruct((m // seq * KV_W, seq), BF16),
              pl.BlockSpec((KV_W, tm), lambda i, j: (i // tiles_per_seq, i % tiles_per_seq)))
    vt, = _proj(u, w_arr, w_idx, [COL_V], KV_W, tm, KV_W, _epi_transpose, 1, outs=[vt_out], name="proj_v")
    return q, k, vt


def _mixer(u, q, k_all, v_all, wts, w_in, w_mgate, layer, batch, seq, tm, cast=None, cast_small=False):
    w_arr, w_idx = w_in
    w_br, br_idx = [wts["w_br_attn"], wts["w_br_conv"], wts["w_br_gm"]], layer
    w_o = (wts["w_out"], layer)
    agate, = _proj(u, w_arr, w_idx, [COL_AGATE], ATTN_W, tm, WIDE_TN, _epi_silu, 1, name="proj_agate")
    z, cgate, *w_o_b = _proj(u, w_arr, w_idx, [COL_CB, COL_CC, COL_CH, COL_CGATE], CONV_W, tm, 256, _epi_conv,
                             2, side_cast=[w_o] if cast_small else (), name="proj_conv")
    ug, gv, *w_br_b = _proj(u, w_arr, w_idx, [COL_GU, COL_GV, COL_GGATE], GM_W, tm, 256, _epi_gm, 2,
                            side_cast=[(w, layer) for w in w_br] if cast_small else (), name="proj_gm")
    if cast_small:
        w_br, br_idx, w_o = w_br_b, 0, (w_o_b[0], 0)
    w_copy = None
    if len(w_mgate) == 2:
        gates, w_copy = _proj(u, w_mgate[0], w_mgate[1], [COL_MGATE], N_BRANCH * D_MODEL, tm, WIDE_TN // 2,
                              _epi_sigmoid, 1, emit_w=True, name="proj_mgate")
    else:
        gates, = _proj(u, w_mgate[0], w_mgate[1], [w_mgate[2]], N_BRANCH * D_MODEL, tm, WIDE_TN,
                       _epi_sigmoid, 1, name="proj_mgate")
    ya, *w_cast = _attention(q, k_all, v_all, agate, wts["score_bound"][layer], batch, min(seq, 512), cast=cast)
    yc, yg = _mix(z, cgate, ug, gv, wts["conv_w"][layer], wts["gm_ln_g"][layer], wts["gm_ln_b"][layer],
                  wts["gm_ws"][layer], wts["gm_bias"][layer], seq, min(seq, 512))
    merged = _merge(ya, yc, yg, *w_br, gates, br_idx, tm)
    return merged, (w_cast[0] if w_cast else None), w_copy, w_o


def kernel(x, c, ctx, c_ctx, w_ada, b_ada, w_in, q_norm, k_norm, conv_w, gm_ln_g, gm_ln_b, gm_ws, gm_b,
           w_br_attn, w_br_conv, w_br_gm, w_out, ln_g, ln_b):
    batch, seq, _ = x.shape
    ctx_len = ctx.shape[1]
    tm, tm_c = 1024, ctx_len

    w_in_l = (w_in[:1, :, :COL_MGATE].astype(BF16), 0)
    w_mgate = (w_in, 0)
    wts = {
        "w_br_attn": w_br_attn, "w_br_conv": w_br_conv,
        "w_br_gm": w_br_gm, "w_out": w_out, "conv_w": conv_w, "gm_ln_g": gm_ln_g, "gm_ln_b": gm_ln_b,
        "gm_ws": gm_ws.astype(BF16),
        "gm_bias": jnp.broadcast_to(gm_b[:, :, :, None], gm_b.shape + (GM_GROUP_W,)),
        "score_bound": [_score_bound(q_norm[l], k_norm[l]) for l in range(DEPTH)],
    }

    cond = jnp.zeros((MOD_ROWS, D_MODEL), F32).at[:batch].set(c).at[batch].set(c_ctx)
    mod = _modulation(cond, w_ada, b_ada)
    mod = mod.reshape(DEPTH, MOD_ROWS, 1, 3, D_MODEL)
    sh, sc, gt = mod[:, :, :, 0], mod[:, :, :, 1], mod[:, :, :, 2]

    lat_row = lambda r: r // seq
    ctx_row = lambda r: batch

    rope = _rope_tables(seq)
    xr = x.reshape(batch * seq, D_MODEL)
    cr = ctx.reshape(batch * ctx_len, D_MODEL)

    u, = _ln_mod(xr, lat_row, mod=(sc[0], sh[0]), emit_x=False)
    u_c, = _ln_mod(cr, ctx_row, mod=(sc[0], sh[0]), emit_x=False)

    for layer in range(DEPTH):
        last = layer == DEPTH - 1
        q_c, k_c, vt_c = _qkv(u_c, w_in_l, layer, q_norm, k_norm, None, ctx_len, tm_c, want_q=not last)
        if not last:
            merged_c, _, w_copy, w_o = _mixer(u_c, q_c, k_c, vt_c, wts, w_in_l, w_mgate, layer, batch, ctx_len,
                                              batch * ctx_len)
            if w_copy is not None:
                w_mgate = (w_copy, 0, 0)
            pre_c = _outproj(merged_c, w_o[0], cr, gt[layer], w_o[1], ctx_row, batch * ctx_len)
            cr, u_c = _ln_mod(pre_c, ctx_row, ln=(ln_g[layer], ln_b[layer]),
                              mod=(sc[layer + 1], sh[layer + 1]))
        q, k, vt = _qkv(u, w_in_l, layer, q_norm, k_norm, rope, seq, tm)
        k_all = jnp.concatenate([k.reshape(batch, seq, KV_W), k_c.reshape(batch, ctx_len, KV_W)], axis=1)
        k_all = k_all.reshape(batch * (seq + ctx_len), KV_W)
        vt_all = jnp.concatenate([vt, vt_c], axis=1)
        merged, w_next, _, w_o = _mixer(u, q, k_all, vt_all, wts, w_in_l, w_mgate, layer, batch, seq, tm,
                                        cast=None if last else (w_in, layer + 1), cast_small=True)
        w_in_l, w_mgate = (w_next, 0), (w_next, 0, COL_MGATE)
        pre = _outproj(merged, w_o[0], xr, gt[layer], w_o[1], lat_row, tm)
        if last:
            xr, = _ln_mod(pre, lat_row, ln=(ln_g[layer], ln_b[layer]))
        else:
            xr, u = _ln_mod(pre, lat_row, ln=(ln_g[layer], ln_b[layer]), mod=(sc[layer + 1], sh[layer + 1]))
    return xr.reshape(batch, seq, D_MODEL)
```

```python
import functools

import jax
import jax.numpy as jnp
from jax import lax
from jax.experimental import pallas as pl
from jax.experimental.pallas import tpu as pltpu

D_MODEL = 4096
DEPTH = 2
GRID_W = 64
HEAD_DIM = 128
N_Q_HEADS = 16
N_KV_HEADS = 4
GQA_GROUP = N_Q_HEADS // N_KV_HEADS
ATTN_W = N_Q_HEADS * HEAD_DIM
KV_W = N_KV_HEADS * HEAD_DIM
GROUP_W = GQA_GROUP * HEAD_DIM
AXIS_DIM = HEAD_DIM // 2
ROPE_THETA = 10000.0
CONV_W = 1024
GM_W = 1024
GM_GROUPS = 8
GM_GROUP_W = GM_W // GM_GROUPS
GM_CHUNK = 128
N_BRANCH = 3
EPS = 1e-6
ALPHA = (2.0 * DEPTH) ** 0.25

COL_Q = 0
COL_K = ATTN_W
COL_V = COL_K + KV_W
COL_AGATE = COL_V + KV_W
COL_CB = COL_AGATE + ATTN_W
COL_CC = COL_CB + CONV_W
COL_CH = COL_CC + CONV_W
COL_CGATE = COL_CH + CONV_W
COL_GU = COL_CGATE + CONV_W
COL_GV = COL_GU + GM_W
COL_GGATE = COL_GV + GM_W
COL_MGATE = COL_GGATE + GM_W
IN_COLS = COL_MGATE + N_BRANCH * D_MODEL

LANES = 128
MOD_ROWS = 8
ONES_ROWS = 16
SUB_M = 512
WIDE_TN = 1024
V7X_VMEM_LIMIT = 56 * 1024 * 1024

BF16 = jnp.bfloat16
F32 = jnp.float32


def _params(sem, flags=None):
    return pltpu.CompilerParams(dimension_semantics=sem, vmem_limit_bytes=V7X_VMEM_LIMIT, flags=flags)


def _sigmoid(x):
    return 0.5 * jnp.tanh(0.5 * x) + 0.5


def _silu(x):
    return x * _sigmoid(x)


def _modulation_body(c_ref, w_ref, b_ref, o_ref):
    s = _silu(c_ref[...]).astype(BF16)
    o_ref[...] = jnp.dot(s, w_ref[...].astype(BF16), preferred_element_type=F32) + b_ref[...]


def _modulation(cond, w_ada, b_ada, tn=512):
    n = 3 * D_MODEL
    return pl.pallas_call(
        _modulation_body,
        grid=(DEPTH, n // tn),
        in_specs=[
            pl.BlockSpec((MOD_ROWS, D_MODEL), lambda l, j: (0, 0)),
            pl.BlockSpec((None, D_MODEL, tn), lambda l, j: (l, 0, j)),
            pl.BlockSpec((None, 1, tn), lambda l, j: (l, 0, j)),
        ],
        out_specs=pl.BlockSpec((None, MOD_ROWS, tn), lambda l, j: (l, 0, j)),
        out_shape=jax.ShapeDtypeStruct((DEPTH, MOD_ROWS, n), F32),
        compiler_params=_params(("arbitrary", "arbitrary")),
        name="modulation",
    )(cond, w_ada, b_ada.reshape(DEPTH, 1, n))


def _ln_mod_body(*refs, do_ln, emit_x, emit_u):
    it = iter(refs)
    x_ref = next(it)
    g_ref = b_ref = sc_ref = sh_ref = None
    if do_ln:
        g_ref, b_ref = next(it), next(it)
    if emit_u:
        sc_ref, sh_ref = next(it), next(it)
    y = x_ref[...]
    if do_ln:
        mu = jnp.mean(y, axis=-1, keepdims=True)
        d = y - mu
        var = jnp.mean(d * d, axis=-1, keepdims=True)
        y = d * lax.rsqrt(var + EPS) * g_ref[...] + b_ref[...]
    if emit_x:
        next(it)[...] = y
    if emit_u:
        next(it)[...] = (y * (1.0 + sc_ref[...]) + sh_ref[...]).astype(BF16)


def _ln_mod(x, row_fn, ln=None, mod=None, emit_x=True, tr=512):
    m = x.shape[0]
    do_ln, emit_u = ln is not None, mod is not None
    row = pl.BlockSpec((tr, D_MODEL), lambda i: (i, 0))
    vec = pl.BlockSpec((1, D_MODEL), lambda i: (0, 0))
    mvec = pl.BlockSpec((None, 1, D_MODEL), lambda i: (row_fn(i * tr), 0, 0))
    ins, specs = [x], [row]
    if do_ln:
        ins += [ln[0].reshape(1, D_MODEL), ln[1].reshape(1, D_MODEL)]
        specs += [vec, vec]
    if emit_u:
        ins += [mod[0], mod[1]]
        specs += [mvec, mvec]
    out_shape, out_specs = [], []
    if emit_x:
        out_shape.append(jax.ShapeDtypeStruct((m, D_MODEL), F32))
        out_specs.append(row)
    if emit_u:
        out_shape.append(jax.ShapeDtypeStruct((m, D_MODEL), BF16))
        out_specs.append(row)
    return pl.pallas_call(
        functools.partial(_ln_mod_body, do_ln=do_ln, emit_x=emit_x, emit_u=emit_u),
        grid=(m // tr,),
        in_specs=specs,
        out_specs=out_specs,
        out_shape=out_shape,
        compiler_params=_params(("arbitrary",)),
        name="ln_mod",
    )(*ins)


def _proj_body(u_ref, *refs, n_w, n_extra, n_side, epilogue, emit_w):
    w_refs = refs[:n_w]
    extra_refs = refs[n_w:n_w + n_extra]
    side_refs = refs[n_w + n_extra:n_w + n_extra + n_side]
    out_refs = refs[n_w + n_extra + n_side:]
    if n_side:
        out_refs, cast_refs = out_refs[:-n_side], out_refs[-n_side:]
        for src, dst in zip(side_refs, cast_refs):
            dst[...] = src[...].astype(BF16)
    weights = [w[...].astype(BF16) for w in w_refs]
    if emit_w:
        out_refs, wb_refs = out_refs[:-n_w], out_refs[-n_w:]
        for wb_ref, w in zip(wb_refs, weights):
            wb_ref[...] = w
    for rows in _row_blocks(u_ref.shape[0]):
        u = u_ref[rows, :]
        accs = [jnp.dot(u, w, preferred_element_type=F32) for w in weights]
        epilogue(accs, extra_refs, out_refs, rows)


def _proj(u, weights, width, tm, tn, epilogue, n_out, extras=(), outs=None, tn_out=None, emit_w=False,
          side_cast=(), name="proj"):
    m, k = u.shape
    tn_out = tn if tn_out is None else tn_out
    ni, nj = m // tm, width // tn_out
    in_specs = [pl.BlockSpec((tm, k), lambda i, j: (i, 0))]
    for _, idx, c0, step in weights:
        step = tn if step is None else step
        in_specs.append(pl.BlockSpec((None, k, tn),
                                     lambda i, j, idx=idx, cb=c0 // tn, st=step // tn: (idx, 0, cb + j * st)))
    in_specs += [spec for _, spec in extras]
    if outs is None:
        outs = [(jax.ShapeDtypeStruct((m, width), BF16),
                 pl.BlockSpec((tm, tn_out), lambda i, j: (i, j)))] * n_out
    outs = list(outs)
    if emit_w:
        outs += [(jax.ShapeDtypeStruct((1, k, nj * tn), BF16),
                  pl.BlockSpec((None, k, tn), lambda i, j: (0, 0, j)))] * len(weights)
    for arr, idx in side_cast:
        _, r, c = arr.shape
        slab = c // (ni * nj)
        in_specs.append(pl.BlockSpec((None, r, slab), lambda i, j, idx=idx: (idx, 0, i * nj + j)))
        outs.append((jax.ShapeDtypeStruct((1, r, c), BF16),
                     pl.BlockSpec((None, r, slab), lambda i, j: (0, 0, i * nj + j))))
    return pl.pallas_call(
        functools.partial(_proj_body, n_w=len(weights), n_extra=len(extras), n_side=len(side_cast),
                          epilogue=epilogue, emit_w=emit_w),
        grid=(ni, nj),
        in_specs=in_specs,
        out_specs=[spec for _, spec in outs],
        out_shape=[shape for shape, _ in outs],
        compiler_params=_params(("arbitrary", "arbitrary")),
        name=name,
    )(u, *[w[0] for w in weights], *[a for a, _ in extras], *[a for a, _ in side_cast])


def _row_blocks(tm):
    step = min(tm, SUB_M)
    return [slice(r, r + step) for r in range(0, tm, step)]


def _epi_transpose(accs, extras, outs, rows):
    outs[0][:, rows] = accs[0].T.astype(BF16)


def _epi_transpose_seqs(accs, extras, outs, rows, seq):
    for s in range((rows.stop - rows.start) // seq):
        b = rows.start // seq + s
        outs[0][b * KV_W:(b + 1) * KV_W, :] = accs[0][s * seq:(s + 1) * seq, :].T.astype(BF16)


def _epi_silu(accs, extras, outs, rows):
    outs[0][rows, :] = _silu(accs[0]).astype(BF16)


def _epi_sigmoid(accs, extras, outs, rows):
    outs[0][rows, :] = _sigmoid(accs[0]).astype(BF16)


def _epi_conv(accs, extras, outs, rows):
    c_b, c_c, c_h, c_gate = accs
    outs[0][rows, :] = (c_c * c_h).astype(BF16)
    outs[1][rows, :] = (c_b * _silu(c_gate)).astype(BF16)


def _epi_gm(accs, extras, outs, rows):
    g_u, g_v, g_gate = accs
    outs[0][rows, :] = (g_u * _silu(g_gate)).astype(BF16)
    outs[1][rows, :] = g_v.astype(BF16)


def _epi_norm_rope(accs, extras, outs, rows, rope, scale):
    gain = extras[0][...]
    mix = extras[1][...]
    if rope:
        cos, sin = extras[2][rows, :], extras[3][rows, :]
    col = 0
    for acc in accs:
        for h in range(acc.shape[1] // HEAD_DIM):
            xh = acc[:, h * HEAD_DIM:(h + 1) * HEAD_DIM]
            y = xh * gain
            lhs = jnp.concatenate([y, xh * xh], axis=1).astype(BF16)
            r = jnp.dot(lhs, mix, preferred_element_type=F32)
            rstd = lax.rsqrt(r[:, HEAD_DIM:] * (1.0 / HEAD_DIM) + EPS)
            if rope:
                out = (y * cos + r[:, :HEAD_DIM] * sin) * rstd
            else:
                out = y * (rstd * scale)
            outs[0][rows, col:col + HEAD_DIM] = out.astype(BF16)
            col += HEAD_DIM


def _swap_sumsq_matrix():
    j = jnp.arange(HEAD_DIM)
    half = AXIS_DIM // 2
    partner = jnp.where(j % AXIS_DIM < half, j + half, j - half)
    perm = (j[:, None] == partner[None, :]).astype(F32)
    zero = jnp.zeros((HEAD_DIM, HEAD_DIM), F32)
    top = jnp.concatenate([perm, zero], axis=1)
    bottom = jnp.concatenate([zero, jnp.ones((HEAD_DIM, HEAD_DIM), F32)], axis=1)
    return jnp.concatenate([top, bottom], axis=0).astype(BF16)


def _rope_tables(n):
    t = jnp.arange(n)
    pos = jnp.stack([t // GRID_W, t % GRID_W], axis=-1).astype(F32)
    freqs = ROPE_THETA ** (-jnp.arange(0, AXIS_DIM, 2, dtype=F32) / AXIS_DIM)
    ang = pos[:, :, None] * freqs
    cos, sin = jnp.cos(ang), jnp.sin(ang)
    cos = jnp.concatenate([cos, cos], axis=-1).reshape(n, HEAD_DIM)
    sin = jnp.concatenate([-sin, sin], axis=-1).reshape(n, HEAD_DIM)
    return cos, sin


def _attn_body(bound_ref, q_ref, k_ref, vt_ref, gate_ref, *rest, tq, skv, tk):
    if len(rest) == 3:
        w_ref, o_ref, wb_ref = rest
        wb_ref[...] = w_ref[...].astype(BF16)
    else:
        o_ref, = rest
    chunks = [(start, min(tk, skv - start)) for start in range(0, skv, tk)]

    def run(shifted):
        qs = jnp.concatenate([q_ref[:, g * HEAD_DIM:(g + 1) * HEAD_DIM] for g in range(GQA_GROUP)], axis=0)

        def scores(start, size):
            return lax.dot_general(k_ref[start:start + size, :], qs, (((1,), (1,)), ((), ())),
                                   preferred_element_type=F32)

        m = acc = None
        st_next = scores(*chunks[0])
        for n, (start, size) in enumerate(chunks):
            st, st_next = st_next, (scores(*chunks[n + 1]) if n + 1 < len(chunks) else None)
            if shifted:
                m_chunk = jnp.max(st, axis=0, keepdims=True)
                m_new = m_chunk if m is None else jnp.maximum(m, m_chunk)
                st = st - m_new
            p = jnp.exp2(st).astype(BF16)
            vt_ones = jnp.concatenate([vt_ref[:, start:start + size], jnp.ones((ONES_ROWS, size), BF16)],
                                      axis=0)
            pv = jnp.dot(vt_ones, p, preferred_element_type=F32)
            if acc is None:
                acc = pv
            elif shifted:
                acc = jnp.exp2(m - m_new) * acc + pv
            else:
                acc = acc + pv
            if shifted:
                m = m_new
        o = (acc[:HEAD_DIM] / acc[HEAD_DIM:HEAD_DIM + 1]).T
        for g in range(GQA_GROUP):
            sl = slice(g * HEAD_DIM, (g + 1) * HEAD_DIM)
            o_ref[:, sl] = (o[g * tq:(g + 1) * tq] * gate_ref[:, sl].astype(F32)).astype(BF16)

    in_range = bound_ref[0, 0] < SCORE_BOUND_LIMIT
    pl.when(in_range)(functools.partial(run, False))
    pl.when(jnp.logical_not(in_range))(functools.partial(run, True))


def _attention(q, k, vt, gate, score_bound, batch, tq, tk=256, cast=None):
    sq = q.shape[0] // batch
    skv = k.shape[0] // batch
    nq = sq // tq
    qspec = pl.BlockSpec((tq, GROUP_W), lambda b, h, i: (b * nq + i, h))
    kspec = pl.BlockSpec((skv, HEAD_DIM), lambda b, h, i: (b, h))
    vspec = pl.BlockSpec((HEAD_DIM, skv), lambda b, h, i: (b * N_KV_HEADS + h, 0))
    ins = [score_bound, q, k, vt, gate]
    in_specs = [pl.BlockSpec(memory_space=pltpu.SMEM), qspec, kspec, vspec, qspec]
    out_shape, out_specs = [jax.ShapeDtypeStruct(q.shape, BF16)], [qspec]
    if cast is not None:
        w, layer = cast
        _, kdim, ndim = w.shape
        slab = ndim // (batch * N_KV_HEADS * nq)
        step = lambda b, h, i: (b * N_KV_HEADS + h) * nq + i
        ins.append(w)
        in_specs.append(pl.BlockSpec((None, kdim, slab), lambda b, h, i: (layer, 0, step(b, h, i))))
        out_shape.append(jax.ShapeDtypeStruct((1, kdim, ndim), BF16))
        out_specs.append(pl.BlockSpec((None, kdim, slab), lambda b, h, i: (0, 0, step(b, h, i))))
    return pl.pallas_call(
        functools.partial(_attn_body, tq=tq, skv=skv, tk=tk),
        grid=(batch, N_KV_HEADS, nq),
        in_specs=in_specs,
        out_specs=out_specs,
        out_shape=out_shape,
        compiler_params=_params(("arbitrary", "arbitrary", "arbitrary")),
        name="attention",
    )(*ins)


def _mix_body(z_ref, zp_ref, zn_ref, cg_ref, ug_ref, gv_ref, cw_ref, lg_ref, lb_ref, ws_ref, bias_ref,
              yc_ref, yg_ref, *, tr, tiles_per_seq):
    i = pl.program_id(0)
    z = z_ref[...].astype(F32)
    rows = lax.broadcasted_iota(jnp.int32, z.shape, 0)
    prev_row = jnp.where(i % tiles_per_seq == 0, 0.0, zp_ref[7:8, :].astype(F32))
    next_row = jnp.where((i + 1) % tiles_per_seq == 0, 0.0, zn_ref[0:1, :].astype(F32))
    z_prev = jnp.where(rows == 0, prev_row, pltpu.roll(z, 1, 0))
    z_next = jnp.where(rows == tr - 1, next_row, pltpu.roll(z, tr - 1, 0))
    conv = cw_ref[0:1, :] * z_prev + cw_ref[1:2, :] * z + cw_ref[2:3, :] * z_next
    yc_ref[...] = (cg_ref[...].astype(F32) * conv).astype(BF16)

    for c in range(tr // GM_CHUNK):
        rs = slice(c * GM_CHUNK, (c + 1) * GM_CHUNK)
        gv = gv_ref[rs, :].astype(F32)
        mu = jnp.mean(gv, axis=-1, keepdims=True)
        d = gv - mu
        var = jnp.mean(d * d, axis=-1, keepdims=True)
        vn = (d * lax.rsqrt(var + EPS) * lg_ref[...] + lb_ref[...]).astype(BF16)
        for g in range(GM_GROUPS):
            cs = slice(g * GM_GROUP_W, (g + 1) * GM_GROUP_W)
            s = jnp.dot(ws_ref[g], vn[:, cs], preferred_element_type=F32) + bias_ref[g]
            yg_ref[rs, cs] = (ug_ref[rs, cs].astype(F32) * s).astype(BF16)


def _mix(z, cgate, ug, gv, conv_w, ln_g, ln_b, ws, bias, seq, tr):
    m = z.shape[0]
    row = pl.BlockSpec((tr, CONV_W), lambda i: (i, 0))
    r8 = tr // 8
    prev8 = pl.BlockSpec((8, CONV_W), lambda i: (jnp.maximum(i * r8 - 1, 0), 0))
    next8 = pl.BlockSpec((8, CONV_W), lambda i: (jnp.minimum((i + 1) * r8, m // 8 - 1), 0))
    vec = pl.BlockSpec((1, GM_W), lambda i: (0, 0))
    full3 = pl.BlockSpec((GM_GROUPS, GM_CHUNK, GM_CHUNK), lambda i: (0, 0, 0))
    return pl.pallas_call(
        functools.partial(_mix_body, tr=tr, tiles_per_seq=seq // tr),
        grid=(m // tr,),
        in_specs=[row, prev8, next8, row, row, row,
                  pl.BlockSpec((3, CONV_W), lambda i: (0, 0)), vec, vec, full3, full3],
        out_specs=[row, row],
        out_shape=[jax.ShapeDtypeStruct((m, CONV_W), BF16), jax.ShapeDtypeStruct((m, GM_W), BF16)],
        compiler_params=_params(("arbitrary",)),
        name="conv_gmlp",
    )(z, z, z, cgate, ug, gv, conv_w, ln_g.reshape(1, GM_W), ln_b.reshape(1, GM_W), ws, bias)


def _merge_body(ya_ref, yc_ref, yg_ref, wa_ref, wc_ref, wg_ref, ga_ref, gc_ref, gg_ref, o_ref):
    weights = [w_ref[...].astype(BF16) for w_ref in (wa_ref, wc_ref, wg_ref)]
    for rows in _row_blocks(o_ref.shape[0]):
        acc = None
        for gate_ref, y_ref, w in zip((ga_ref, gc_ref, gg_ref), (ya_ref, yc_ref, yg_ref), weights):
            term = gate_ref[rows, :].astype(F32) * jnp.dot(y_ref[rows, :], w, preferred_element_type=F32)
            acc = term if acc is None else acc + term
        o_ref[rows, :] = acc.astype(BF16)


def _merge(ya, yc, yg, w_a, w_c, w_g, gates, layer, tm, tn=512):
    m = ya.shape[0]
    nj = D_MODEL // tn

    def lhs(width):
        return pl.BlockSpec((tm, width), lambda i, j: (i, 0))

    def wspec(width):
        return pl.BlockSpec((None, width, tn), lambda i, j: (layer, 0, j))

    def gspec(branch):
        return pl.BlockSpec((tm, tn), lambda i, j: (i, branch * nj + j))

    return pl.pallas_call(
        _merge_body,
        grid=(m // tm, nj),
        in_specs=[lhs(ATTN_W), lhs(CONV_W), lhs(GM_W), wspec(ATTN_W), wspec(CONV_W), wspec(GM_W),
                  gspec(0), gspec(1), gspec(2)],
        out_specs=pl.BlockSpec((tm, tn), lambda i, j: (i, j)),
        out_shape=jax.ShapeDtypeStruct((m, D_MODEL), BF16),
        compiler_params=_params(("arbitrary", "arbitrary")),
        name="merge",
    )(ya, yc, yg, w_a, w_c, w_g, gates, gates, gates)


def _outproj_body(m_ref, w_ref, x_ref, gt_ref, o_ref):
    w = w_ref[...].astype(BF16)
    for rows in _row_blocks(o_ref.shape[0]):
        acc = jnp.dot(m_ref[rows, :], w, preferred_element_type=F32)
        o_ref[rows, :] = ALPHA * x_ref[rows, :] + gt_ref[...] * acc


def _outproj(merged, w_out, x, gt, layer, row_fn, tm, tn=512):
    m = merged.shape[0]
    return pl.pallas_call(
        _outproj_body,
        grid=(m // tm, D_MODEL // tn),
        in_specs=[
            pl.BlockSpec((tm, D_MODEL), lambda i, j: (i, 0)),
            pl.BlockSpec((None, D_MODEL, tn), lambda i, j: (layer, 0, j)),
            pl.BlockSpec((tm, tn), lambda i, j: (i, j)),
            pl.BlockSpec((None, 1, tn), lambda i, j: (row_fn(i * tm), 0, j)),
        ],
        out_specs=pl.BlockSpec((tm, tn), lambda i, j: (i, j)),
        out_shape=jax.ShapeDtypeStruct((m, D_MODEL), F32),
        compiler_params=_params(("arbitrary", "arbitrary")),
        name="outproj",
    )(merged, w_out, x, gt)


Q_SCALE = HEAD_DIM ** -0.5 * 1.4426950408889634
SCORE_BOUND_LIMIT = 60.0
SCORE_BOUND_SLACK = 1.05


def _score_bound(q_gain, k_gain):
    bound = SCORE_BOUND_SLACK * HEAD_DIM * Q_SCALE * jnp.max(jnp.abs(q_gain)) * jnp.max(jnp.abs(k_gain))
    return bound.reshape(1, 1).astype(F32)


_GROUPS = {"q": [COL_Q], "k": [COL_K], "v": [COL_V], "agate": [COL_AGATE],
           "conv": [COL_CB, COL_CC, COL_CH, COL_CGATE], "gm": [COL_GU, COL_GV, COL_GGATE], "mgate": [COL_MGATE]}
EMIT_TN = 512


def _entries(src, name, tn, parts=1):
    out = []
    for b, start in enumerate(_GROUPS[name]):
        arr, idx, base = (src[0], src[1], start) if isinstance(src, tuple) else (src[name][b], 0, 0)
        out += [(arr, idx, base + s * tn, parts * tn) for s in range(parts)]
    return out


def _qkv(u, src, layer, q_norm, k_norm, rope, seq, tm, want_q=True, emit=False):
    m = u.shape[0]
    tiles_per_seq = seq // tm
    half = 2 * HEAD_DIM
    gain = pl.BlockSpec((1, HEAD_DIM), lambda i, j: (0, 0))
    mix = pl.BlockSpec((2 * HEAD_DIM, 2 * HEAD_DIM), lambda i, j: (0, 0))
    tab = pl.BlockSpec((tm, HEAD_DIM), lambda i, j: (i % tiles_per_seq, 0))

    def extras(g, scale):
        ex = [(g.reshape(1, HEAD_DIM), gain), (_swap_sumsq_matrix(), mix)]
        if rope is not None:
            ex += [(rope[0] * scale, tab), (rope[1] * scale, tab)]
        return ex

    def tiles(parts):
        return (EMIT_TN, 1) if emit else (half, parts)

    q, copies = None, {}
    if want_q:
        tn, parts = tiles(WIDE_TN // half)
        epi_q = functools.partial(_epi_norm_rope, rope=rope is not None, scale=Q_SCALE)
        q, *copies["q"] = _proj(u, _entries(src, "q", tn, parts), ATTN_W, tm, tn, epi_q, 1,
                                extras(q_norm[layer], Q_SCALE), tn_out=tn * parts, emit_w=emit, name="proj_q")
    tn, parts = tiles(2)
    epi_k = functools.partial(_epi_norm_rope, rope=rope is not None, scale=1.0)
    k, *copies["k"] = _proj(u, _entries(src, "k", tn, parts), KV_W, tm, tn, epi_k, 1,
                            extras(k_norm[layer], 1.0), tn_out=tn * parts, emit_w=emit, name="proj_k")
    vt_shape = jax.ShapeDtypeStruct((m // seq * KV_W, seq), BF16)
    if tm > seq:
        assert m == tm and tm % seq == 0
        vt_out = (vt_shape, pl.BlockSpec(vt_shape.shape, lambda i, j: (0, 0)))
        epi_v = functools.partial(_epi_transpose_seqs, seq=seq)
    else:
        vt_out = (vt_shape, pl.BlockSpec((KV_W, tm), lambda i, j: (i // tiles_per_seq, i % tiles_per_seq)))
        epi_v = _epi_transpose
    vt, *copies["v"] = _proj(u, _entries(src, "v", KV_W), KV_W, tm, KV_W, epi_v, 1, outs=[vt_out],
                             emit_w=emit, name="proj_v")
    return q, k, vt, copies


def _mixer(u, q, k_all, v_all, wts, src, layer, batch, seq, tm, cast=None, cast_small=False, emit=False):
    w_br, br_idx = [wts["w_br_attn"], wts["w_br_conv"], wts["w_br_gm"]], layer
    w_o = (wts["w_out"], layer)
    wide = EMIT_TN if emit else WIDE_TN
    narrow = LANES if emit else 2 * LANES
    copies = {}
    agate, *copies["agate"] = _proj(u, _entries(src, "agate", wide), ATTN_W, tm, wide, _epi_silu, 1,
                                    emit_w=emit, name="proj_agate")
    z, cgate, *rest = _proj(u, _entries(src, "conv", narrow), CONV_W, tm, narrow, _epi_conv, 2, emit_w=emit,
                            side_cast=[w_o] if cast_small else (), name="proj_conv")
    copies["conv"], w_o_b = (rest, []) if emit else ([], rest)
    ug, gv, *rest = _proj(u, _entries(src, "gm", narrow), GM_W, tm, narrow, _epi_gm, 2, emit_w=emit,
                          side_cast=[(w, layer) for w in w_br] if cast_small else (), name="proj_gm")
    copies["gm"], w_br_b = (rest, []) if emit else ([], rest)
    if cast_small:
        w_br, br_idx, w_o = w_br_b, 0, (w_o_b[0], 0)
    gates, *copies["mgate"] = _proj(u, _entries(src, "mgate", wide), N_BRANCH * D_MODEL, tm, wide,
                                    _epi_sigmoid, 1, emit_w=emit, name="proj_mgate")
    ya, *w_cast = _attention(q, k_all, v_all, agate, wts["score_bound"][layer], batch, min(seq, 512), cast=cast)
    yc, yg = _mix(z, cgate, ug, gv, wts["conv_w"][layer], wts["gm_ln_g"][layer], wts["gm_ln_b"][layer],
                  wts["gm_ws"][layer], wts["gm_bias"][layer], seq, min(seq, 512))
    merged = _merge(ya, yc, yg, *w_br, gates, br_idx, tm)
    return merged, (w_cast[0] if w_cast else None), copies, w_o


def kernel(x, c, ctx, c_ctx, w_ada, b_ada, w_in, q_norm, k_norm, conv_w, gm_ln_g, gm_ln_b, gm_ws, gm_b,
           w_br_attn, w_br_conv, w_br_gm, w_out, ln_g, ln_b):
    batch, seq, _ = x.shape
    ctx_len = ctx.shape[1]
    tm, tm_c = 1024, batch * ctx_len

    src = (w_in, 0)
    wts = {
        "w_br_attn": w_br_attn, "w_br_conv": w_br_conv,
        "w_br_gm": w_br_gm, "w_out": w_out, "conv_w": conv_w, "gm_ln_g": gm_ln_g, "gm_ln_b": gm_ln_b,
        "gm_ws": gm_ws.astype(BF16),
        "gm_bias": jnp.broadcast_to(gm_b[:, :, :, None], gm_b.shape + (GM_GROUP_W,)),
        "score_bound": [_score_bound(q_norm[l], k_norm[l]) for l in range(DEPTH)],
    }

    cond = jnp.zeros((MOD_ROWS, D_MODEL), F32).at[:batch].set(c).at[batch].set(c_ctx)
    mod = _modulation(cond, w_ada, b_ada)
    mod = mod.reshape(DEPTH, MOD_ROWS, 1, 3, D_MODEL)
    sh, sc, gt = mod[:, :, :, 0], mod[:, :, :, 1], mod[:, :, :, 2]

    lat_row = lambda r: r // seq
    ctx_row = lambda r: batch

    rope = _rope_tables(seq)
    xr = x.reshape(batch * seq, D_MODEL)
    cr = ctx.reshape(batch * ctx_len, D_MODEL)

    u, = _ln_mod(xr, lat_row, mod=(sc[0], sh[0]), emit_x=False)
    u_c, = _ln_mod(cr, ctx_row, mod=(sc[0], sh[0]), emit_x=False)

    for layer in range(DEPTH):
        last = layer == DEPTH - 1
        emit = layer == 0
        q_c, k_c, vt_c, copies = _qkv(u_c, src, layer, q_norm, k_norm, None, ctx_len, tm_c, want_q=not last,
                                      emit=emit)
        if not last:
            merged_c, _, more, w_o = _mixer(u_c, q_c, k_c, vt_c, wts, src, layer, batch, ctx_len,
                                            batch * ctx_len, emit=emit)
            if emit:
                src = {**copies, **more}
            pre_c = _outproj(merged_c, w_o[0], cr, gt[layer], w_o[1], ctx_row, batch * ctx_len)
            cr, u_c = _ln_mod(pre_c, ctx_row, ln=(ln_g[layer], ln_b[layer]),
                              mod=(sc[layer + 1], sh[layer + 1]))
        q, k, vt, _ = _qkv(u, src, layer, q_norm, k_norm, rope, seq, tm)
        k_all = jnp.concatenate([k.reshape(batch, seq, KV_W), k_c.reshape(batch, ctx_len, KV_W)], axis=1)
        k_all = k_all.reshape(batch * (seq + ctx_len), KV_W)
        vt_all = jnp.concatenate([vt, vt_c], axis=1)
        merged, w_next, _, w_o = _mixer(u, q, k_all, vt_all, wts, src, layer, batch, seq, tm,
                                        cast=None if last else (w_in, layer + 1), cast_small=True)
        src = (w_next, 0)
        pre = _outproj(merged, w_o[0], xr, gt[layer], w_o[1], lat_row, tm)
        if last:
            xr, = _ln_mod(pre, lat_row, ln=(ln_g[layer], ln_b[layer]))
        else:
            xr, u = _ln_mod(pre, lat_row, ln=(ln_g[layer], ln_b[layer]), mod=(sc[layer + 1], sh[layer + 1]))
    return xr.reshape(batch, seq, D_MODEL)
```

```python
import functools

import jax
import jax.numpy as jnp
from jax import lax
from jax.experimental import pallas as pl
from jax.experimental.pallas import tpu as pltpu

D_MODEL = 4096
DEPTH = 2
GRID_W = 64
HEAD_DIM = 128
N_Q_HEADS = 16
N_KV_HEADS = 4
GQA_GROUP = N_Q_HEADS // N_KV_HEADS
ATTN_W = N_Q_HEADS * HEAD_DIM
KV_W = N_KV_HEADS * HEAD_DIM
GROUP_W = GQA_GROUP * HEAD_DIM
AXIS_DIM = HEAD_DIM // 2
ROPE_THETA = 10000.0
CONV_W = 1024
GM_W = 1024
GM_GROUPS = 8
GM_GROUP_W = GM_W // GM_GROUPS
GM_CHUNK = 128
N_BRANCH = 3
EPS = 1e-6
ALPHA = (2.0 * DEPTH) ** 0.25

COL_Q = 0
COL_K = ATTN_W
COL_V = COL_K + KV_W
COL_AGATE = COL_V + KV_W
COL_CB = COL_AGATE + ATTN_W
COL_CC = COL_CB + CONV_W
COL_CH = COL_CC + CONV_W
COL_CGATE = COL_CH + CONV_W
COL_GU = COL_CGATE + CONV_W
COL_GV = COL_GU + GM_W
COL_GGATE = COL_GV + GM_W
COL_MGATE = COL_GGATE + GM_W
IN_COLS = COL_MGATE + N_BRANCH * D_MODEL

LANES = 128
MOD_ROWS = 8
ONES_ROWS = 16
SUB_M = 512
WIDE_TN = 1024
V7X_VMEM_LIMIT = 56 * 1024 * 1024

BF16 = jnp.bfloat16
F32 = jnp.float32


def _params(sem, flags=None):
    return pltpu.CompilerParams(dimension_semantics=sem, vmem_limit_bytes=V7X_VMEM_LIMIT, flags=flags)


def _sigmoid(x):
    return 0.5 * jnp.tanh(0.5 * x) + 0.5


def _silu(x):
    return x * _sigmoid(x)


def _modulation_body(c_ref, w_ref, b_ref, o_ref):
    s = _silu(c_ref[...]).astype(BF16)
    o_ref[...] = jnp.dot(s, w_ref[...].astype(BF16), preferred_element_type=F32) + b_ref[...]


def _modulation(cond, w_ada, b_ada, tn=512):
    n = 3 * D_MODEL
    return pl.pallas_call(
        _modulation_body,
        grid=(DEPTH, n // tn),
        in_specs=[
            pl.BlockSpec((MOD_ROWS, D_MODEL), lambda l, j: (0, 0)),
            pl.BlockSpec((None, D_MODEL, tn), lambda l, j: (l, 0, j)),
            pl.BlockSpec((None, 1, tn), lambda l, j: (l, 0, j)),
        ],
        out_specs=pl.BlockSpec((None, MOD_ROWS, tn), lambda l, j: (l, 0, j)),
        out_shape=jax.ShapeDtypeStruct((DEPTH, MOD_ROWS, n), F32),
        compiler_params=_params(("arbitrary", "arbitrary")),
        name="modulation",
    )(cond, w_ada, b_ada.reshape(DEPTH, 1, n))


def _ln_mod_body(*refs, do_ln, emit_x, emit_u):
    it = iter(refs)
    x_ref = next(it)
    g_ref = b_ref = sc_ref = sh_ref = None
    if do_ln:
        g_ref, b_ref = next(it), next(it)
    if emit_u:
        sc_ref, sh_ref = next(it), next(it)
    y = x_ref[...]
    if do_ln:
        mu = jnp.mean(y, axis=-1, keepdims=True)
        d = y - mu
        var = jnp.mean(d * d, axis=-1, keepdims=True)
        y = d * lax.rsqrt(var + EPS) * g_ref[...] + b_ref[...]
    if emit_x:
        next(it)[...] = y
    if emit_u:
        next(it)[...] = (y * (1.0 + sc_ref[...]) + sh_ref[...]).astype(BF16)


def _ln_mod(x, row_fn, ln=None, mod=None, emit_x=True, tr=512):
    m = x.shape[0]
    do_ln, emit_u = ln is not None, mod is not None
    row = pl.BlockSpec((tr, D_MODEL), lambda i: (i, 0))
    vec = pl.BlockSpec((1, D_MODEL), lambda i: (0, 0))
    mvec = pl.BlockSpec((None, 1, D_MODEL), lambda i: (row_fn(i * tr), 0, 0))
    ins, specs = [x], [row]
    if do_ln:
        ins += [ln[0].reshape(1, D_MODEL), ln[1].reshape(1, D_MODEL)]
        specs += [vec, vec]
    if emit_u:
        ins += [mod[0], mod[1]]
        specs += [mvec, mvec]
    out_shape, out_specs = [], []
    if emit_x:
        out_shape.append(jax.ShapeDtypeStruct((m, D_MODEL), F32))
        out_specs.append(row)
    if emit_u:
        out_shape.append(jax.ShapeDtypeStruct((m, D_MODEL), BF16))
        out_specs.append(row)
    return pl.pallas_call(
        functools.partial(_ln_mod_body, do_ln=do_ln, emit_x=emit_x, emit_u=emit_u),
        grid=(m // tr,),
        in_specs=specs,
        out_specs=out_specs,
        out_shape=out_shape,
        compiler_params=_params(("arbitrary",)),
        name="ln_mod",
    )(*ins)


def _proj_body(u_ref, *refs, n_w, n_extra, n_side, epilogue, emit_w):
    w_refs = refs[:n_w]
    extra_refs = refs[n_w:n_w + n_extra]
    side_refs = refs[n_w + n_extra:n_w + n_extra + n_side]
    out_refs = refs[n_w + n_extra + n_side:]
    if n_side:
        out_refs, cast_refs = out_refs[:-n_side], out_refs[-n_side:]
        for src, dst in zip(side_refs, cast_refs):
            dst[...] = src[...].astype(BF16)
    weights = [w[...].astype(BF16) for w in w_refs]
    if emit_w:
        out_refs, wb_refs = out_refs[:-n_w], out_refs[-n_w:]
        for wb_ref, w in zip(wb_refs, weights):
            wb_ref[...] = w
    for rows in _row_blocks(u_ref.shape[0]):
        u = u_ref[rows, :]
        accs = [jnp.dot(u, w, preferred_element_type=F32) for w in weights]
        epilogue(accs, extra_refs, out_refs, rows)


def _proj(u, weights, width, tm, tn, epilogue, n_out, extras=(), outs=None, tn_out=None, emit_w=False,
          side_cast=(), name="proj"):
    m, k = u.shape
    tn_out = tn if tn_out is None else tn_out
    ni, nj = m // tm, width // tn_out
    in_specs = [pl.BlockSpec((tm, k), lambda i, j: (i, 0))]
    for _, idx, c0, step in weights:
        step = tn if step is None else step
        in_specs.append(pl.BlockSpec((None, k, tn),
                                     lambda i, j, idx=idx, cb=c0 // tn, st=step // tn: (idx, 0, cb + j * st)))
    in_specs += [spec for _, spec in extras]
    if outs is None:
        outs = [(jax.ShapeDtypeStruct((m, width), BF16),
                 pl.BlockSpec((tm, tn_out), lambda i, j: (i, j)))] * n_out
    outs = list(outs)
    if emit_w:
        outs += [(jax.ShapeDtypeStruct((1, k, nj * tn), BF16),
                  pl.BlockSpec((None, k, tn), lambda i, j: (0, 0, j)))] * len(weights)
    for arr, idx in side_cast:
        _, r, c = arr.shape
        slab = c // (ni * nj)
        in_specs.append(pl.BlockSpec((None, r, slab), lambda i, j, idx=idx: (idx, 0, i * nj + j)))
        outs.append((jax.ShapeDtypeStruct((1, r, c), BF16),
                     pl.BlockSpec((None, r, slab), lambda i, j: (0, 0, i * nj + j))))
    return pl.pallas_call(
        functools.partial(_proj_body, n_w=len(weights), n_extra=len(extras), n_side=len(side_cast),
                          epilogue=epilogue, emit_w=emit_w),
        grid=(ni, nj),
        in_specs=in_specs,
        out_specs=[spec for _, spec in outs],
        out_shape=[shape for shape, _ in outs],
        compiler_params=_params(("arbitrary", "arbitrary")),
        name=name,
    )(u, *[w[0] for w in weights], *[a for a, _ in extras], *[a for a, _ in side_cast])


def _row_blocks(tm):
    step = min(tm, SUB_M)
    return [slice(r, r + step) for r in range(0, tm, step)]


def _epi_transpose(accs, extras, outs, rows):
    outs[0][:, rows] = accs[0].T.astype(BF16)


def _epi_transpose_seqs(accs, extras, outs, rows, seq):
    for s in range((rows.stop - rows.start) // seq):
        b = rows.start // seq + s
        outs[0][b * KV_W:(b + 1) * KV_W, :] = accs[0][s * seq:(s + 1) * seq, :].T.astype(BF16)


def _epi_silu(accs, extras, outs, rows):
    outs[0][rows, :] = _silu(accs[0]).astype(BF16)


def _epi_sigmoid(accs, extras, outs, rows):
    outs[0][rows, :] = _sigmoid(accs[0]).astype(BF16)


def _epi_conv(accs, extras, outs, rows):
    c_b, c_c, c_h, c_gate = accs
    outs[0][rows, :] = (c_c * c_h).astype(BF16)
    outs[1][rows, :] = (c_b * _silu(c_gate)).astype(BF16)


def _epi_gm(accs, extras, outs, rows):
    g_u, g_v, g_gate = accs
    outs[0][rows, :] = (g_u * _silu(g_gate)).astype(BF16)
    outs[1][rows, :] = g_v.astype(BF16)


def _epi_norm_rope(accs, extras, outs, rows, rope, scale):
    gain = extras[0][...]
    mix = extras[1][...]
    if rope:
        cos, sin = extras[2][rows, :], extras[3][rows, :]
    col = 0
    for acc in accs:
        for h in range(acc.shape[1] // HEAD_DIM):
            xh = acc[:, h * HEAD_DIM:(h + 1) * HEAD_DIM]
            y = xh * gain
            lhs = jnp.concatenate([y, xh * xh], axis=1).astype(BF16)
            r = jnp.dot(lhs, mix, preferred_element_type=F32)
            rstd = lax.rsqrt(r[:, HEAD_DIM:] * (1.0 / HEAD_DIM) + EPS)
            if rope:
                out = (y * cos + r[:, :HEAD_DIM] * sin) * rstd
            else:
                out = y * (rstd * scale)
            outs[0][rows, col:col + HEAD_DIM] = out.astype(BF16)
            col += HEAD_DIM


def _swap_sumsq_matrix():
    j = jnp.arange(HEAD_DIM)
    half = AXIS_DIM // 2
    partner = jnp.where(j % AXIS_DIM < half, j + half, j - half)
    perm = (j[:, None] == partner[None, :]).astype(F32)
    zero = jnp.zeros((HEAD_DIM, HEAD_DIM), F32)
    top = jnp.concatenate([perm, zero], axis=1)
    bottom = jnp.concatenate([zero, jnp.ones((HEAD_DIM, HEAD_DIM), F32)], axis=1)
    return jnp.concatenate([top, bottom], axis=0).astype(BF16)


def _rope_tables(n):
    t = jnp.arange(n)
    pos = jnp.stack([t // GRID_W, t % GRID_W], axis=-1).astype(F32)
    freqs = ROPE_THETA ** (-jnp.arange(0, AXIS_DIM, 2, dtype=F32) / AXIS_DIM)
    ang = pos[:, :, None] * freqs
    cos, sin = jnp.cos(ang), jnp.sin(ang)
    cos = jnp.concatenate([cos, cos], axis=-1).reshape(n, HEAD_DIM)
    sin = jnp.concatenate([-sin, sin], axis=-1).reshape(n, HEAD_DIM)
    return cos, sin


def _attn_body(bound_ref, q_ref, k_ref, vt_ref, gate_ref, *rest, tq, skv, tk):
    if len(rest) == 3:
        w_ref, o_ref, wb_ref = rest
        wb_ref[...] = w_ref[...].astype(BF16)
    else:
        o_ref, = rest
    chunks = [(start, min(tk, skv - start)) for start in range(0, skv, tk)]

    def run(shifted):
        qs = jnp.concatenate([q_ref[:, g * HEAD_DIM:(g + 1) * HEAD_DIM] for g in range(GQA_GROUP)], axis=0)

        def scores(start, size):
            return lax.dot_general(k_ref[start:start + size, :], qs, (((1,), (1,)), ((), ())),
                                   preferred_element_type=F32)

        m = acc = None
        st_next = scores(*chunks[0])
        for n, (start, size) in enumerate(chunks):
            st, st_next = st_next, (scores(*chunks[n + 1]) if n + 1 < len(chunks) else None)
            if shifted:
                m_chunk = jnp.max(st, axis=0, keepdims=True)
                m_new = m_chunk if m is None else jnp.maximum(m, m_chunk)
                st = st - m_new
            else:
                st = st - bound
            p = jnp.exp2(st).astype(BF16)
            vt_ones = jnp.concatenate([vt_ref[:, start:start + size], jnp.ones((ONES_ROWS, size), BF16)],
                                      axis=0)
            pv = jnp.dot(vt_ones, p, preferred_element_type=F32)
            if acc is None:
                acc = pv
            elif shifted:
                acc = jnp.exp2(m - m_new) * acc + pv
            else:
                acc = acc + pv
            if shifted:
                m = m_new
        o = (acc[:HEAD_DIM] / acc[HEAD_DIM:HEAD_DIM + 1]).T
        for g in range(GQA_GROUP):
            sl = slice(g * HEAD_DIM, (g + 1) * HEAD_DIM)
            o_ref[:, sl] = (o[g * tq:(g + 1) * tq] * gate_ref[:, sl].astype(F32)).astype(BF16)

    bound = bound_ref[0, 0]
    in_range = bound < SCORE_BOUND_LIMIT
    pl.when(in_range)(functools.partial(run, False))
    pl.when(jnp.logical_not(in_range))(functools.partial(run, True))


def _attention(q, k, vt, gate, score_bound, batch, tq, tk=256, cast=None):
    sq = q.shape[0] // batch
    skv = k.shape[0] // batch
    nq = sq // tq
    qspec = pl.BlockSpec((tq, GROUP_W), lambda b, h, i: (b * nq + i, h))
    kspec = pl.BlockSpec((skv, HEAD_DIM), lambda b, h, i: (b, h))
    vspec = pl.BlockSpec((HEAD_DIM, skv), lambda b, h, i: (b * N_KV_HEADS + h, 0))
    ins = [score_bound, q, k, vt, gate]
    in_specs = [pl.BlockSpec(memory_space=pltpu.SMEM), qspec, kspec, vspec, qspec]
    out_shape, out_specs = [jax.ShapeDtypeStruct(q.shape, BF16)], [qspec]
    if cast is not None:
        w, layer = cast
        _, kdim, ndim = w.shape
        slab = ndim // (batch * N_KV_HEADS * nq)
        step = lambda b, h, i: (b * N_KV_HEADS + h) * nq + i
        ins.append(w)
        in_specs.append(pl.BlockSpec((None, kdim, slab), lambda b, h, i: (layer, 0, step(b, h, i))))
        out_shape.append(jax.ShapeDtypeStruct((1, kdim, ndim), BF16))
        out_specs.append(pl.BlockSpec((None, kdim, slab), lambda b, h, i: (0, 0, step(b, h, i))))
    return pl.pallas_call(
        functools.partial(_attn_body, tq=tq, skv=skv, tk=tk),
        grid=(batch, N_KV_HEADS, nq),
        in_specs=in_specs,
        out_specs=out_specs,
        out_shape=out_shape,
        compiler_params=_params(("arbitrary", "arbitrary", "arbitrary")),
        name="attention",
    )(*ins)


def _mix_body(z_ref, zp_ref, zn_ref, cg_ref, ug_ref, gv_ref, cw_ref, lg_ref, lb_ref, ws_ref, bias_ref,
              yc_ref, yg_ref, *, tr, tiles_per_seq):
    i = pl.program_id(0)
    z = z_ref[...].astype(F32)
    rows = lax.broadcasted_iota(jnp.int32, z.shape, 0)
    prev_row = jnp.where(i % tiles_per_seq == 0, 0.0, zp_ref[7:8, :].astype(F32))
    next_row = jnp.where((i + 1) % tiles_per_seq == 0, 0.0, zn_ref[0:1, :].astype(F32))
    z_prev = jnp.where(rows == 0, prev_row, pltpu.roll(z, 1, 0))
    z_next = jnp.where(rows == tr - 1, next_row, pltpu.roll(z, tr - 1, 0))
    conv = cw_ref[0:1, :] * z_prev + cw_ref[1:2, :] * z + cw_ref[2:3, :] * z_next
    yc_ref[...] = (cg_ref[...].astype(F32) * conv).astype(BF16)

    for c in range(tr // GM_CHUNK):
        rs = slice(c * GM_CHUNK, (c + 1) * GM_CHUNK)
        gv = gv_ref[rs, :].astype(F32)
        mu = jnp.mean(gv, axis=-1, keepdims=True)
        d = gv - mu
        var = jnp.mean(d * d, axis=-1, keepdims=True)
        vn = (d * lax.rsqrt(var + EPS) * lg_ref[...] + lb_ref[...]).astype(BF16)
        for g in range(GM_GROUPS):
            cs = slice(g * GM_GROUP_W, (g + 1) * GM_GROUP_W)
            s = jnp.dot(ws_ref[g], vn[:, cs], preferred_element_type=F32) + bias_ref[g]
            yg_ref[rs, cs] = (ug_ref[rs, cs].astype(F32) * s).astype(BF16)


def _mix(z, cgate, ug, gv, conv_w, ln_g, ln_b, ws, bias, seq, tr):
    m = z.shape[0]
    row = pl.BlockSpec((tr, CONV_W), lambda i: (i, 0))
    r8 = tr // 8
    prev8 = pl.BlockSpec((8, CONV_W), lambda i: (jnp.maximum(i * r8 - 1, 0), 0))
    next8 = pl.BlockSpec((8, CONV_W), lambda i: (jnp.minimum((i + 1) * r8, m // 8 - 1), 0))
    vec = pl.BlockSpec((1, GM_W), lambda i: (0, 0))
    full3 = pl.BlockSpec((GM_GROUPS, GM_CHUNK, GM_CHUNK), lambda i: (0, 0, 0))
    return pl.pallas_call(
        functools.partial(_mix_body, tr=tr, tiles_per_seq=seq // tr),
        grid=(m // tr,),
        in_specs=[row, prev8, next8, row, row, row,
                  pl.BlockSpec((3, CONV_W), lambda i: (0, 0)), vec, vec, full3, full3],
        out_specs=[row, row],
        out_shape=[jax.ShapeDtypeStruct((m, CONV_W), BF16), jax.ShapeDtypeStruct((m, GM_W), BF16)],
        compiler_params=_params(("arbitrary",)),
        name="conv_gmlp",
    )(z, z, z, cgate, ug, gv, conv_w, ln_g.reshape(1, GM_W), ln_b.reshape(1, GM_W), ws, bias)


def _merge_body(ya_ref, yc_ref, yg_ref, wa_ref, wc_ref, wg_ref, ga_ref, gc_ref, gg_ref, o_ref):
    weights = [w_ref[...].astype(BF16) for w_ref in (wa_ref, wc_ref, wg_ref)]
    for rows in _row_blocks(o_ref.shape[0]):
        acc = None
        for gate_ref, y_ref, w in zip((ga_ref, gc_ref, gg_ref), (ya_ref, yc_ref, yg_ref), weights):
            term = gate_ref[rows, :].astype(F32) * jnp.dot(y_ref[rows, :], w, preferred_element_type=F32)
            acc = term if acc is None else acc + term
        o_ref[rows, :] = acc.astype(BF16)


def _merge(ya, yc, yg, w_a, w_c, w_g, gates, layer, tm, tn=512):
    m = ya.shape[0]
    nj = D_MODEL // tn

    def lhs(width):
        return pl.BlockSpec((tm, width), lambda i, j: (i, 0))

    def wspec(width):
        return pl.BlockSpec((None, width, tn), lambda i, j: (layer, 0, j))

    def gspec(branch):
        return pl.BlockSpec((tm, tn), lambda i, j: (i, branch * nj + j))

    return pl.pallas_call(
        _merge_body,
        grid=(m // tm, nj),
        in_specs=[lhs(ATTN_W), lhs(CONV_W), lhs(GM_W), wspec(ATTN_W), wspec(CONV_W), wspec(GM_W),
                  gspec(0), gspec(1), gspec(2)],
        out_specs=pl.BlockSpec((tm, tn), lambda i, j: (i, j)),
        out_shape=jax.ShapeDtypeStruct((m, D_MODEL), BF16),
        compiler_params=_params(("arbitrary", "arbitrary")),
        name="merge",
    )(ya, yc, yg, w_a, w_c, w_g, gates, gates, gates)


def _outproj_body(m_ref, w_ref, x_ref, gt_ref, o_ref):
    w = w_ref[...].astype(BF16)
    for rows in _row_blocks(o_ref.shape[0]):
        acc = jnp.dot(m_ref[rows, :], w, preferred_element_type=F32)
        o_ref[rows, :] = ALPHA * x_ref[rows, :] + gt_ref[...] * acc


def _outproj(merged, w_out, x, gt, layer, row_fn, tm, tn=512):
    m = merged.shape[0]
    return pl.pallas_call(
        _outproj_body,
        grid=(m // tm, D_MODEL // tn),
        in_specs=[
            pl.BlockSpec((tm, D_MODEL), lambda i, j: (i, 0)),
            pl.BlockSpec((None, D_MODEL, tn), lambda i, j: (layer, 0, j)),
            pl.BlockSpec((tm, tn), lambda i, j: (i, j)),
            pl.BlockSpec((None, 1, tn), lambda i, j: (row_fn(i * tm), 0, j)),
        ],
        out_specs=pl.BlockSpec((tm, tn), lambda i, j: (i, j)),
        out_shape=jax.ShapeDtypeStruct((m, D_MODEL), F32),
        compiler_params=_params(("arbitrary", "arbitrary")),
        name="outproj",
    )(merged, w_out, x, gt)


Q_SCALE = HEAD_DIM ** -0.5 * 1.4426950408889634
SCORE_BOUND_LIMIT = 60.0
SCORE_BOUND_SLACK = 1.05


def _score_bound(q_gain, k_gain):
    bound = SCORE_BOUND_SLACK * HEAD_DIM * Q_SCALE * jnp.max(jnp.abs(q_gain)) * jnp.max(jnp.abs(k_gain))
    return bound.reshape(1, 1).astype(F32)


_GROUPS = {"q": [COL_Q], "k": [COL_K], "v": [COL_V], "agate": [COL_AGATE],
           "conv": [COL_CB, COL_CC, COL_CH, COL_CGATE], "gm": [COL_GU, COL_GV, COL_GGATE], "mgate": [COL_MGATE]}
EMIT_TN = 512


def _entries(src, name, tn, parts=1):
    out = []
    for b, start in enumerate(_GROUPS[name]):
        arr, idx, base = (src[0], src[1], start) if isinstance(src, tuple) else (src[name][b], 0, 0)
        out += [(arr, idx, base + s * tn, parts * tn) for s in range(parts)]
    return out


def _qkv(u, src, layer, q_norm, k_norm, rope, seq, tm, want_q=True, emit=False):
    m = u.shape[0]
    tiles_per_seq = seq // tm
    half = 2 * HEAD_DIM
    gain = pl.BlockSpec((1, HEAD_DIM), lambda i, j: (0, 0))
    mix = pl.BlockSpec((2 * HEAD_DIM, 2 * HEAD_DIM), lambda i, j: (0, 0))
    tab = pl.BlockSpec((tm, HEAD_DIM), lambda i, j: (i % tiles_per_seq, 0))

    def extras(g, scale):
        ex = [(g.reshape(1, HEAD_DIM), gain), (_swap_sumsq_matrix(), mix)]
        if rope is not None:
            ex += [(rope[0] * scale, tab), (rope[1] * scale, tab)]
        return ex

    def tiles(parts):
        return (EMIT_TN, 1) if emit else (half, parts)

    q, copies = None, {}
    if want_q:
        tn, parts = tiles(WIDE_TN // half)
        epi_q = functools.partial(_epi_norm_rope, rope=rope is not None, scale=Q_SCALE)
        q, *copies["q"] = _proj(u, _entries(src, "q", tn, parts), ATTN_W, tm, tn, epi_q, 1,
                                extras(q_norm[layer], Q_SCALE), tn_out=tn * parts, emit_w=emit, name="proj_q")
    tn, parts = tiles(2)
    epi_k = functools.partial(_epi_norm_rope, rope=rope is not None, scale=1.0)
    k, *copies["k"] = _proj(u, _entries(src, "k", tn, parts), KV_W, tm, tn, epi_k, 1,
                            extras(k_norm[layer], 1.0), tn_out=tn * parts, emit_w=emit, name="proj_k")
    vt_shape = jax.ShapeDtypeStruct((m // seq * KV_W, seq), BF16)
    if tm > seq:
        assert m == tm and tm % seq == 0
        vt_out = (vt_shape, pl.BlockSpec(vt_shape.shape, lambda i, j: (0, 0)))
        epi_v = functools.partial(_epi_transpose_seqs, seq=seq)
    else:
        vt_out = (vt_shape, pl.BlockSpec((KV_W, tm), lambda i, j: (i // tiles_per_seq, i % tiles_per_seq)))
        epi_v = _epi_transpose
    vt, *copies["v"] = _proj(u, _entries(src, "v", KV_W), KV_W, tm, KV_W, epi_v, 1, outs=[vt_out],
                             emit_w=emit, name="proj_v")
    return q, k, vt, copies


def _mixer(u, q, k_all, v_all, wts, src, layer, batch, seq, tm, cast=None, cast_small=False, emit=False):
    w_br, br_idx = [wts["w_br_attn"], wts["w_br_conv"], wts["w_br_gm"]], layer
    w_o = (wts["w_out"], layer)
    wide = EMIT_TN if emit else WIDE_TN
    narrow = LANES if emit else 2 * LANES
    copies = {}
    agate, *copies["agate"] = _proj(u, _entries(src, "agate", wide), ATTN_W, tm, wide, _epi_silu, 1,
                                    emit_w=emit, name="proj_agate")
    z, cgate, *rest = _proj(u, _entries(src, "conv", narrow), CONV_W, tm, narrow, _epi_conv, 2, emit_w=emit,
                            side_cast=[w_o] if cast_small else (), name="proj_conv")
    copies["conv"], w_o_b = (rest, []) if emit else ([], rest)
    ug, gv, *rest = _proj(u, _entries(src, "gm", narrow), GM_W, tm, narrow, _epi_gm, 2, emit_w=emit,
                          side_cast=[(w, layer) for w in w_br] if cast_small else (), name="proj_gm")
    copies["gm"], w_br_b = (rest, []) if emit else ([], rest)
    if cast_small:
        w_br, br_idx, w_o = w_br_b, 0, (w_o_b[0], 0)
    gates, *copies["mgate"] = _proj(u, _entries(src, "mgate", wide), N_BRANCH * D_MODEL, tm, wide,
                                    _epi_sigmoid, 1, emit_w=emit, name="proj_mgate")
    ya, *w_cast = _attention(q, k_all, v_all, agate, wts["score_bound"][layer], batch, min(seq, 512), cast=cast)
    yc, yg = _mix(z, cgate, ug, gv, wts["conv_w"][layer], wts["gm_ln_g"][layer], wts["gm_ln_b"][layer],
                  wts["gm_ws"][layer], wts["gm_bias"][layer], seq, min(seq, 512))
    merged = _merge(ya, yc, yg, *w_br, gates, br_idx, tm)
    return merged, (w_cast[0] if w_cast else None), copies, w_o


def kernel(x, c, ctx, c_ctx, w_ada, b_ada, w_in, q_norm, k_norm, conv_w, gm_ln_g, gm_ln_b, gm_ws, gm_b,
           w_br_attn, w_br_conv, w_br_gm, w_out, ln_g, ln_b):
    batch, seq, _ = x.shape
    ctx_len = ctx.shape[1]
    tm, tm_c = 1024, batch * ctx_len

    src = (w_in, 0)
    wts = {
        "w_br_attn": w_br_attn, "w_br_conv": w_br_conv,
        "w_br_gm": w_br_gm, "w_out": w_out, "conv_w": conv_w, "gm_ln_g": gm_ln_g, "gm_ln_b": gm_ln_b,
        "gm_ws": gm_ws.astype(BF16),
        "gm_bias": jnp.broadcast_to(gm_b[:, :, :, None], gm_b.shape + (GM_GROUP_W,)),
        "score_bound": [_score_bound(q_norm[l], k_norm[l]) for l in range(DEPTH)],
    }

    cond = jnp.zeros((MOD_ROWS, D_MODEL), F32).at[:batch].set(c).at[batch].set(c_ctx)
    mod = _modulation(cond, w_ada, b_ada)
    mod = mod.reshape(DEPTH, MOD_ROWS, 1, 3, D_MODEL)
    sh, sc, gt = mod[:, :, :, 0], mod[:, :, :, 1], mod[:, :, :, 2]

    lat_row = lambda r: r // seq
    ctx_row = lambda r: batch

    rope = _rope_tables(seq)
    xr = x.reshape(batch * seq, D_MODEL)
    cr = ctx.reshape(batch * ctx_len, D_MODEL)

    u, = _ln_mod(xr, lat_row, mod=(sc[0], sh[0]), emit_x=False)
    u_c, = _ln_mod(cr, ctx_row, mod=(sc[0], sh[0]), emit_x=False)

    for layer in range(DEPTH):
        last = layer == DEPTH - 1
        emit = layer == 0
        q_c, k_c, vt_c, copies = _qkv(u_c, src, layer, q_norm, k_norm, None, ctx_len, tm_c, want_q=not last,
                                      emit=emit)
        if not last:
            merged_c, _, more, w_o = _mixer(u_c, q_c, k_c, vt_c, wts, src, layer, batch, ctx_len,
                                            batch * ctx_len, emit=emit)
            if emit:
                src = {**copies, **more}
            pre_c = _outproj(merged_c, w_o[0], cr, gt[layer], w_o[1], ctx_row, batch * ctx_len)
            cr, u_c = _ln_mod(pre_c, ctx_row, ln=(ln_g[layer], ln_b[layer]),
                              mod=(sc[layer + 1], sh[layer + 1]))
        q, k, vt, _ = _qkv(u, src, layer, q_norm, k_norm, rope, seq, tm)
        k_all = jnp.concatenate([k.reshape(batch, seq, KV_W), k_c.reshape(batch, ctx_len, KV_W)], axis=1)
        k_all = k_all.reshape(batch * (seq + ctx_len), KV_W)
        vt_all = jnp.concatenate([vt, vt_c], axis=1)
        merged, w_next, _, w_o = _mixer(u, q, k_all, vt_all, wts, src, layer, batch, seq, tm,
                                        cast=None if last else (w_in, layer + 1), cast_small=True)
        src = (w_next, 0)
        pre = _outproj(merged, w_o[0], xr, gt[layer], w_o[1], lat_row, tm)
        if last:
            xr, = _ln_mod(pre, lat_row, ln=(ln_g[layer], ln_b[layer]))
        else:
            xr, u = _ln_mod(pre, lat_row, ln=(ln_g[layer], ln_b[layer]), mod=(sc[layer + 1], sh[layer + 1]))
    return xr.reshape(batch, seq, D_MODEL)
```
